```python
import math
import jax, jax.numpy as jnp
from jax import lax
import numpy as np

D_MODEL = 2048
BATCH = 16
SEQ = 2048
DEPTH = 1
DEC_BATCH = 2
DEC_SEQ = 16384
PAST_LEN = 128

ATT_WIDTH = D_MODEL // 2
POOL_WIDTH = D_MODEL - ATT_WIDTH
N_ATT_HEADS = 8
ATT_HEAD_DIM = ATT_WIDTH // (2 * N_ATT_HEADS)
ATT_V_DIM = 2 * ATT_HEAD_DIM
POOL_WINDOWS = (2, 4, 8, 16)
N_POOL_GROUPS = len(POOL_WINDOWS)
POOL_GROUP_DIM = POOL_WIDTH // N_POOL_GROUPS
IN_PROJ_DIM = 3 * ATT_WIDTH + POOL_WIDTH
N_BUCKETS = 32
MAX_DISTANCE = 128
Q_BLOCK = 128
N_KEYS = 128
N_EXPERTS = N_KEYS * N_KEYS
PEER_HEADS = 8
PEER_TOPK = 16
PEER_KEY_DIM = 128
PEER_QUERY_DIM = 2 * PEER_KEY_DIM
PEER_CHUNK = 128
NORM_EPS = 1e-6
SUBLN_EPS = 1e-5

kernel_name = "hybrid_pool_diffattn_peer_encoder"


def lambda_init(layer_idx):
    return 0.8 - 0.6 * math.exp(-0.3 * layer_idx)


def rmsnorm(x, w, eps=NORM_EPS):
    xf = x.astype(jnp.float32)
    y = xf * lax.rsqrt(jnp.mean(xf * xf, axis=-1, keepdims=True) + eps)
    return (y * w.astype(jnp.float32)).astype(x.dtype)


def relative_bucket(rel_pos):
    nb = N_BUCKETS // 2
    ret = (rel_pos > 0).astype(jnp.int32) * nb
    n = jnp.abs(rel_pos)
    max_exact = nb // 2
    large = max_exact + (jnp.log(jnp.maximum(n, 1).astype(jnp.float32) / max_exact)
                         / math.log(MAX_DISTANCE / max_exact) * (nb - max_exact)).astype(jnp.int32)
    large = jnp.minimum(large, nb - 1)
    return ret + jnp.where(n < max_exact, n, large)


def multiscale_pool(p, pool_w, pool_scale):
    B, S, _ = p.shape
    pg = p.reshape(B, S, N_POOL_GROUPS, POOL_GROUP_DIM).astype(jnp.float32)
    cs = jnp.concatenate([jnp.zeros((B, 1, N_POOL_GROUPS, POOL_GROUP_DIM), jnp.float32),
                          jnp.cumsum(pg, axis=1)], axis=1)
    t = jnp.arange(S, dtype=jnp.int32)[:, None]
    half = jnp.array([w // 2 for w in POOL_WINDOWS], jnp.int32)[None, :]
    lo = jnp.clip(t - half, 0, S)
    hi = jnp.clip(t + half, 0, S)
    g = jnp.arange(N_POOL_GROUPS, dtype=jnp.int32)[None, :]
    window_sum = cs[:, hi, g, :] - cs[:, lo, g, :]
    count = (hi - lo).astype(jnp.float32)[None, :, :, None]
    mixed = (window_sum / count - pg).astype(p.dtype)
    out = jnp.einsum('bsgc,gcd->bsgd', mixed, pool_w)
    return out.reshape(B, S, POOL_WIDTH) * pool_scale


def diff_attention(q, k, v, lam, rel_bias):
    B, S = q.shape[0], q.shape[1]
    nb = S // Q_BLOCK
    qb = q.reshape(B, nb, Q_BLOCK, N_ATT_HEADS, 2, ATT_HEAD_DIM).transpose(1, 0, 3, 4, 2, 5)
    kt = k.transpose(0, 2, 3, 1, 4)
    vt = v.transpose(0, 2, 1, 3)
    k_pos = jnp.arange(S, dtype=jnp.int32)
    scale = ATT_HEAD_DIM ** -0.5

    def block(args):
        q_blk, start = args
        q_pos = start + jnp.arange(Q_BLOCK, dtype=jnp.int32)
        bucket = relative_bucket(k_pos[None, :] - q_pos[:, None])
        bias = rel_bias[bucket].astype(jnp.float32).transpose(2, 0, 1)
        logits = jnp.einsum('bhiqd,bhikd->bhiqk', q_blk, kt).astype(jnp.float32) * scale \
            + bias[None, :, None]
        probs = jax.nn.softmax(logits, axis=-1)
        weights = probs[:, :, 0] - lam * probs[:, :, 1]
        return jnp.einsum('bhqk,bhkd->bhqd', weights.astype(vt.dtype), vt)

    starts = jnp.arange(nb, dtype=jnp.int32) * Q_BLOCK
    out = lax.map(block, (qb, starts))
    return out.transpose(1, 0, 3, 2, 4).reshape(B, S, N_ATT_HEADS, ATT_V_DIM)


def peer(h, wq, keys, u, v):
    B, S, D = h.shape
    T = B * S
    xt = h.reshape(T, D)
    q = (xt @ wq).reshape(T, PEER_HEADS, 2, PEER_KEY_DIM)
    scores = jnp.einsum('thid,hind->thin', q, keys).astype(jnp.float32)
    s_half, i_half = lax.top_k(scores, PEER_TOPK)
    cand = (s_half[:, :, 0, :, None] + s_half[:, :, 1, None, :]).reshape(T, PEER_HEADS, PEER_TOPK * PEER_TOPK)
    best, flat = lax.top_k(cand, PEER_TOPK)
    i1 = jnp.take_along_axis(i_half[:, :, 0], flat // PEER_TOPK, axis=-1)
    i2 = jnp.take_along_axis(i_half[:, :, 1], flat % PEER_TOPK, axis=-1)
    expert = i1 * N_KEYS + i2
    gates = jax.nn.softmax(best, axis=-1)
    nc = T // PEER_CHUNK
    xc = xt.reshape(nc, PEER_CHUNK, D)
    ec = expert.reshape(nc, PEER_CHUNK, PEER_HEADS, PEER_TOPK)
    gc = gates.reshape(nc, PEER_CHUNK, PEER_HEADS, PEER_TOPK)

    def chunk(args):
        x_c, e_c, g_c = args
        a = jnp.einsum('cd,chkd->chk', x_c, u[e_c]).astype(jnp.float32)
        w = (g_c * jax.nn.gelu(a, approximate=False)).astype(x_c.dtype)
        return jnp.einsum('chk,chkd->cd', w, v[e_c])

    out = lax.map(chunk, (xc, ec, gc))
    return out.reshape(B, S, D)


def trunk(x, rel_bias, norm1_w, w_in, lambda_q1, lambda_k1, lambda_q2, lambda_k2, subln_w,
          pool_w, pool_scale, w_out, norm2_w, peer_wq, peer_keys, peer_u, peer_v, final_norm_w):
    B, S, _ = x.shape
    for l in range(DEPTH):
        lam_init = lambda_init(l)
        h = rmsnorm(x, norm1_w[l])
        proj = h @ w_in[l]
        q = proj[..., :ATT_WIDTH].reshape(B, S, N_ATT_HEADS, 2, ATT_HEAD_DIM)
        k = proj[..., ATT_WIDTH:2 * ATT_WIDTH].reshape(B, S, N_ATT_HEADS, 2, ATT_HEAD_DIM)
        v = proj[..., 2 * ATT_WIDTH:3 * ATT_WIDTH].reshape(B, S, N_ATT_HEADS, ATT_V_DIM)
        p = proj[..., 3 * ATT_WIDTH:]
        lam = (jnp.exp(jnp.sum(lambda_q1[l].astype(jnp.float32) * lambda_k1[l].astype(jnp.float32)))
               - jnp.exp(jnp.sum(lambda_q2[l].astype(jnp.float32) * lambda_k2[l].astype(jnp.float32)))
               + lam_init)
        att = diff_attention(q, k, v, lam, rel_bias)
        att = (rmsnorm(att, subln_w[l], SUBLN_EPS) * (1.0 - lam_init)).reshape(B, S, ATT_WIDTH)
        pooled = multiscale_pool(p, pool_w[l], pool_scale[l])
        x = x + jnp.concatenate([att, pooled], axis=-1) @ w_out[l]
        x = x + peer(rmsnorm(x, norm2_w[l]), peer_wq[l], peer_keys[l], peer_u[l], peer_v[l])
    return rmsnorm(x, final_norm_w)


def setup_inputs(seed: int = 0) -> dict:
    key = jax.random.key(seed)
    ks = jax.random.split(key, 20)
    f32 = jnp.float32
    nrm = lambda k, shape, s: jax.random.normal(k, shape, f32) * s
    return {
        "x_prompt": nrm(ks[0], (BATCH, SEQ, D_MODEL), 1.0),
        "x_sample": nrm(ks[1], (DEC_BATCH, DEC_SEQ, D_MODEL), 1.0),
        "rel_bias": nrm(ks[2], (N_BUCKETS, N_ATT_HEADS), 0.5),
        "norm1_w": 1.0 + nrm(ks[3], (DEPTH, D_MODEL), 0.02),
        "w_in": nrm(ks[4], (DEPTH, D_MODEL, IN_PROJ_DIM), D_MODEL ** -0.5),
        "lambda_q1": nrm(ks[5], (DEPTH, ATT_HEAD_DIM), 0.1),
        "lambda_k1": nrm(ks[6], (DEPTH, ATT_HEAD_DIM), 0.1),
        "lambda_q2": nrm(ks[7], (DEPTH, ATT_HEAD_DIM), 0.1),
        "lambda_k2": nrm(ks[8], (DEPTH, ATT_HEAD_DIM), 0.1),
        "subln_w": 1.0 + nrm(ks[9], (DEPTH, ATT_V_DIM), 0.02),
        "pool_w": nrm(ks[10], (DEPTH, N_POOL_GROUPS, POOL_GROUP_DIM, POOL_GROUP_DIM), POOL_GROUP_DIM ** -0.5),
        "pool_scale": 1.0 + nrm(ks[11], (DEPTH, POOL_WIDTH), 0.1),
        "w_out": nrm(ks[12], (DEPTH, D_MODEL, D_MODEL), D_MODEL ** -0.5),
        "norm2_w": 1.0 + nrm(ks[13], (DEPTH, D_MODEL), 0.02),
        "peer_wq": nrm(ks[14], (DEPTH, D_MODEL, PEER_HEADS * PEER_QUERY_DIM), D_MODEL ** -0.5),
        "peer_keys": nrm(ks[15], (DEPTH, PEER_HEADS, 2, N_KEYS, PEER_KEY_DIM), PEER_KEY_DIM ** -0.5),
        "peer_u": nrm(ks[16], (DEPTH, N_EXPERTS, D_MODEL), D_MODEL ** -0.5),
        "peer_v": nrm(ks[17], (DEPTH, N_EXPERTS, D_MODEL), 0.5),
        "final_norm_w": 1.0 + nrm(ks[18], (D_MODEL,), 0.02),
    }


def reference(x_prompt, x_sample, rel_bias, norm1_w, w_in, lambda_q1, lambda_k1, lambda_q2, lambda_k2,
              subln_w, pool_w, pool_scale, w_out, norm2_w, peer_wq, peer_keys, peer_u, peer_v, final_norm_w):
    y_prompt = trunk(x_prompt, rel_bias, norm1_w, w_in, lambda_q1, lambda_k1, lambda_q2, lambda_k2, subln_w,
                     pool_w, pool_scale, w_out, norm2_w, peer_wq, peer_keys, peer_u, peer_v, final_norm_w)
    y_sample = trunk(x_sample, rel_bias, norm1_w, w_in, lambda_q1, lambda_k1, lambda_q2, lambda_k2, subln_w,
                     pool_w, pool_scale, w_out, norm2_w, peer_wq, peer_keys, peer_u, peer_v, final_norm_w)
    return (y_prompt, y_sample)
```

```python
import functools
import math

import numpy as np
import jax
import jax.numpy as jnp
from jax import lax
from jax.experimental import pallas as pl
from jax.experimental.pallas import tpu as pltpu

f32 = jnp.float32
bf16 = jnp.bfloat16

D_MODEL = 2048
ATT_WIDTH = 1024
N_ATT_HEADS = 8
ATT_HEAD_DIM = 64
ATT_V_DIM = 128
POOL_WIDTH = 1024
POOL_WINDOWS = (2, 4, 8, 16)
POOL_GROUP_DIM = 256
IN_PROJ_DIM = 3 * ATT_WIDTH + POOL_WIDTH
N_BUCKETS = 32
MAX_DISTANCE = 128
N_KEYS = 128
PEER_HEADS = 8
PEER_TOPK = 16
PEER_KEY_DIM = 128
PEER_SLOTS = PEER_HEADS * PEER_TOPK
NORM_EPS = 1e-6
SUBLN_EPS = 1e-5

LANES = 128
BF16_SUBLANES = 16
VMEM_LIMIT = 56 * 1024 * 1024

ATT_TILE = 512
INPROJ_TM, INPROJ_TN = 512, 1024
POOL_TM = 256
ROUTE_TM = 256
PEER_TT = 128
POOL_HALO = BF16_SUBLANES


def _lambda_init(layer_idx):
    return 0.8 - 0.6 * math.exp(-0.3 * layer_idx)


def _rms(x, w, eps):
    ms = jnp.mean(x * x, axis=-1, keepdims=True)
    return x * lax.rsqrt(ms + eps) * w


def _params(*sem):
    return pltpu.CompilerParams(dimension_semantics=sem, vmem_limit_bytes=VMEM_LIMIT)


def _inproj_body(x_ref, nw_ref, w_ref, o_ref, h_ref):
    @pl.when(pl.program_id(1) == 0)
    def _():
        h_ref[...] = _rms(x_ref[...], nw_ref[...], NORM_EPS).astype(bf16)

    o_ref[...] = jnp.dot(h_ref[...], w_ref[...], preferred_element_type=f32).astype(o_ref.dtype)


def _inproj(x2d, norm_w, w_bf):
    T, N = x2d.shape[0], w_bf.shape[1]
    tm, tn = min(INPROJ_TM, T), INPROJ_TN
    return pl.pallas_call(
        _inproj_body,
        out_shape=jax.ShapeDtypeStruct((T, N), bf16),
        grid=(T // tm, N // tn),
        in_specs=[pl.BlockSpec((tm, D_MODEL), lambda i, j: (i, 0)),
                  pl.BlockSpec((1, D_MODEL), lambda i, j: (0, 0)),
                  pl.BlockSpec((D_MODEL, tn), lambda i, j: (0, j))],
        out_specs=pl.BlockSpec((tm, tn), lambda i, j: (i, j)),
        scratch_shapes=[pltpu.VMEM((tm, D_MODEL), bf16)],
        compiler_params=_params("parallel", "arbitrary"),
        name="inproj",
    )(x2d, norm_w, w_bf)


def _bucket_of_rel(rel):
    nb = N_BUCKETS // 2
    ret = (rel > 0).astype(np.int64) * nb
    n = np.abs(rel)
    max_exact = nb // 2
    large = max_exact + (np.log(np.maximum(n, 1).astype(np.float64) / max_exact)
                         / math.log(MAX_DISTANCE / max_exact) * (nb - max_exact)).astype(np.int64)
    large = np.minimum(large, nb - 1)
    return (ret + np.where(n < max_exact, n, large)).astype(np.int32)


def _attn_body(lam_ref, cfar_ref, q_ref, k_ref, v_ref, bias_ref, sw_ref, o_ref, m_ref, l_ref, acc_ref,
               *, lam_scale):
    h, qi, ki = pl.program_id(1), pl.program_id(2), pl.program_id(3)
    nk = pl.num_programs(3)

    @pl.when(ki == 0)
    def _():
        m_ref[...] = jnp.full(m_ref.shape, -jnp.inf, f32)
        l_ref[...] = jnp.zeros(l_ref.shape, f32)
        acc_ref[...] = jnp.zeros(acc_ref.shape, f32)

    q = q_ref[...] * jnp.asarray(ATT_HEAD_DIM ** -0.5, bf16)
    k = k_ref[...]
    v = v_ref[...]
    nt = (((1,), (1,)), ((), ()))
    s_maps = [lax.dot_general(q[:, i * ATT_HEAD_DIM:(i + 1) * ATT_HEAD_DIM],
                              k[:, i * ATT_HEAD_DIM:(i + 1) * ATT_HEAD_DIM], nt,
                              preferred_element_type=f32) for i in range(2)]

    def update(shift_fn):
        for i in range(2):
            s = shift_fn(s_maps[i])
            m_prev = m_ref[i]
            m_new = jnp.maximum(m_prev, jnp.max(s, axis=1, keepdims=True))
            alpha = jnp.exp(m_prev - m_new)
            p = jnp.exp(s - m_new)
            l_ref[i] = alpha * l_ref[i] + jnp.sum(p, axis=1, keepdims=True)
            acc_ref[i] = alpha * acc_ref[i] + jnp.dot(p.astype(bf16), v, preferred_element_type=f32)
            m_ref[i] = m_new

    near = jnp.abs(ki - qi) <= 1

    @pl.when(near)
    def _():
        update(lambda s: s + bias_ref[...])

    @pl.when(jnp.logical_not(near))
    def _():
        c = jnp.where(ki < qi, cfar_ref[h, 0], cfar_ref[h, 1])
        update(lambda s: s + c)

    @pl.when(ki == nk - 1)
    def _():
        lam = lam_ref[0]
        out = acc_ref[0] / l_ref[0] - lam * (acc_ref[1] / l_ref[1])
        o_ref[...] = (_rms(out, sw_ref[...], SUBLN_EPS) * lam_scale).astype(o_ref.dtype)


def _attention(proj3, lam, rel_bias, subln_w, lam_scale):
    B, S, _ = proj3.shape
    t = min(ATT_TILE, S)
    assert S % t == 0 and t >= MAX_DISTANCE
    rel_vec = np.arange(-(2 * t - 1), 2 * t)
    bias_vec = rel_bias[jnp.asarray(_bucket_of_rel(rel_vec))].astype(f32)
    r = np.arange(t)[:, None]
    c = np.arange(t)[None, :]
    idx = np.stack([(c - r) + d * t + (2 * t - 1) for d in (-1, 0, 1)])
    bias_tiles = jnp.transpose(bias_vec[jnp.asarray(idx)], (3, 0, 1, 2))
    far = _bucket_of_rel(np.array([-MAX_DISTANCE, MAX_DISTANCE]))
    cfar = jnp.transpose(rel_bias[jnp.asarray(far)].astype(f32))
    n_heads = N_ATT_HEADS
    return pl.pallas_call(
        functools.partial(_attn_body, lam_scale=lam_scale),
        out_shape=jax.ShapeDtypeStruct((B, S, ATT_WIDTH), bf16),
        grid=(B, n_heads, S // t, S // t),
        in_specs=[pl.BlockSpec(memory_space=pltpu.SMEM),
                  pl.BlockSpec(memory_space=pltpu.SMEM),
                  pl.BlockSpec((None, t, LANES), lambda b, h, qi, ki: (b, qi, h)),
                  pl.BlockSpec((None, t, LANES), lambda b, h, qi, ki: (b, ki, n_heads + h)),
                  pl.BlockSpec((None, t, LANES), lambda b, h, qi, ki: (b, ki, 2 * n_heads + h)),
                  pl.BlockSpec((None, None, t, t), lambda b, h, qi, ki: (h, jnp.clip(ki - qi, -1, 1) + 1, 0, 0)),
                  pl.BlockSpec((1, ATT_V_DIM), lambda b, h, qi, ki: (0, 0))],
        out_specs=pl.BlockSpec((None, t, LANES), lambda b, h, qi, ki: (b, qi, h)),
        scratch_shapes=[pltpu.VMEM((2, t, 1), f32), pltpu.VMEM((2, t, 1), f32),
                        pltpu.VMEM((2, t, ATT_V_DIM), f32)],
        compiler_params=_params("parallel", "parallel", "parallel", "arbitrary"),
        name="diff_attention",
    )(lam, cfar, proj3, proj3, proj3, bias_tiles, subln_w)


def _pool_body(x_ref, att_ref, p_ref, prev_ref, next_ref, pw_ref, ps_ref, wo_ref, o_ref, *, seq, tm):
    i = pl.program_id(0)
    t0 = (i * tm) % seq
    prev_ok = (t0 > 0).astype(f32)
    next_ok = (t0 + tm < seq).astype(f32)
    pm = p_ref[...].astype(f32)
    ext = jnp.concatenate([prev_ref[...].astype(f32) * prev_ok, pm,
                           next_ref[...].astype(f32) * next_ok], axis=0)
    pos = t0 + lax.broadcasted_iota(jnp.int32, (tm, 1), 0)
    pooled = []
    for g, w in enumerate(POOL_WINDOWS):
        cs = slice(g * POOL_GROUP_DIM, (g + 1) * POOL_GROUP_DIM)
        arr = ext[:, cs]
        n = 1
        while n < w:
            rows = arr.shape[0]
            arr = arr[0:rows - n] + arr[n:rows]
            n *= 2
        half = w // 2
        wsum = arr[POOL_HALO - half:POOL_HALO - half + tm]
        count = (jnp.minimum(pos + half, seq) - jnp.maximum(pos - half, 0)).astype(f32)
        mixed = (wsum / count - pm[:, cs]).astype(bf16)
        pooled.append((jnp.dot(mixed, pw_ref[g], preferred_element_type=f32) * ps_ref[:, cs]).astype(bf16))
    cat = jnp.concatenate([att_ref[...]] + pooled, axis=1)
    o_ref[...] = x_ref[...] + jnp.dot(cat, wo_ref[...], preferred_element_type=f32)


def _pool_outproj(x2d, att2d, proj, pool_w_bf, pool_scale, w_out_bf, seq):
    T = x2d.shape[0]
    tm = min(POOL_TM, seq)
    assert seq % tm == 0 and tm % POOL_HALO == 0
    hb = tm // POOL_HALO
    last = T // POOL_HALO - 1
    pcol = 3 * ATT_WIDTH // POOL_WIDTH
    return pl.pallas_call(
        functools.partial(_pool_body, seq=seq, tm=tm),
        out_shape=jax.ShapeDtypeStruct((T, D_MODEL), f32),
        grid=(T // tm,),
        in_specs=[pl.BlockSpec((tm, D_MODEL), lambda i: (i, 0)),
                  pl.BlockSpec((tm, ATT_WIDTH), lambda i: (i, 0)),
                  pl.BlockSpec((tm, POOL_WIDTH), lambda i: (i, pcol)),
                  pl.BlockSpec((POOL_HALO, POOL_WIDTH), lambda i: (jnp.maximum(i * hb - 1, 0), pcol)),
                  pl.BlockSpec((POOL_HALO, POOL_WIDTH), lambda i: (jnp.minimum((i + 1) * hb, last), pcol)),
                  pl.BlockSpec((len(POOL_WINDOWS), POOL_GROUP_DIM, POOL_GROUP_DIM), lambda i: (0, 0, 0)),
                  pl.BlockSpec((1, POOL_WIDTH), lambda i: (0, 0)),
                  pl.BlockSpec((D_MODEL, D_MODEL), lambda i: (0, 0))],
        out_specs=pl.BlockSpec((tm, D_MODEL), lambda i: (i, 0)),
        compiler_params=_params("parallel"),
        name="pool_outproj",
    )(x2d, att2d, proj, proj, proj, pool_w_bf, pool_scale, w_out_bf)


def _topk_rows(s, k):
    n = s.shape[0]
    iota = lax.broadcasted_iota(jnp.int32, s.shape, 0)
    vals, idxs = [], []
    for _ in range(k):
        m = jnp.max(s, axis=0, keepdims=True)
        idx = jnp.minimum(jnp.min(jnp.where(s == m, iota, n), axis=0, keepdims=True), n - 1)
        vals.append(m)
        idxs.append(idx)
        s = jnp.where(iota == idx, -jnp.inf, s)
    return jnp.concatenate(vals, axis=0), jnp.concatenate(idxs, axis=0)


def _select_rows(table, sel):
    out = jnp.zeros(sel.shape, table.dtype)
    for a in range(table.shape[0]):
        out = jnp.where(sel == a, table[a:a + 1, :], out)
    return out


def _route_body(x1_ref, nw_ref, wq_ref, keys_ref, h2_ref, e_ref, g_ref, qp_ref):
    h = pl.program_id(1)

    @pl.when(h == 0)
    def _():
        h2 = _rms(x1_ref[...], nw_ref[...], NORM_EPS)
        h2_ref[...] = h2
        qp = jnp.dot(h2.astype(bf16), wq_ref[...], preferred_element_type=f32).astype(bf16)
        for hh in range(PEER_HEADS):
            qp_ref[hh] = qp[:, hh * 2 * PEER_KEY_DIM:(hh + 1) * 2 * PEER_KEY_DIM]

    q = qp_ref[h]
    nt = (((1,), (1,)), ((), ()))
    halves = []
    for i in range(2):
        st = lax.dot_general(keys_ref[2 * h + i], q[:, i * PEER_KEY_DIM:(i + 1) * PEER_KEY_DIM], nt,
                             preferred_element_type=f32)
        halves.append(_topk_rows(st, PEER_TOPK))
    (s1, i1), (s2, i2) = halves
    cand = jnp.concatenate([s1[a:a + 1, :] + s2 for a in range(PEER_TOPK)], axis=0)
    best, flat = _topk_rows(cand, PEER_TOPK)
    e1 = _select_rows(i1, lax.shift_right_logical(flat, int(math.log2(PEER_TOPK))))
    e2 = _select_rows(i2, lax.bitwise_and(flat, PEER_TOPK - 1))
    e_ref[...] = e1 * N_KEYS + e2
    ex = jnp.exp(best - jnp.max(best, axis=0, keepdims=True))
    g_ref[...] = ex / jnp.sum(ex, axis=0, keepdims=True)


def _peer_route(x1, norm_w, wq_bf, keys_bf):
    T = x1.shape[0]
    tm = min(ROUTE_TM, T)
    return pl.pallas_call(
        _route_body,
        out_shape=(jax.ShapeDtypeStruct((T, D_MODEL), f32),
                   jax.ShapeDtypeStruct((PEER_SLOTS, T), jnp.int32),
                   jax.ShapeDtypeStruct((PEER_SLOTS, T), f32)),
        grid=(T // tm, PEER_HEADS),
        in_specs=[pl.BlockSpec((tm, D_MODEL), lambda i, h: (i, 0)),
                  pl.BlockSpec((1, D_MODEL), lambda i, h: (0, 0)),
                  pl.BlockSpec((D_MODEL, 2 * PEER_KEY_DIM * PEER_HEADS), lambda i, h: (0, 0)),
                  pl.BlockSpec((2 * PEER_HEADS, N_KEYS, PEER_KEY_DIM), lambda i, h: (0, 0, 0))],
        out_specs=(pl.BlockSpec((tm, D_MODEL), lambda i, h: (i, 0)),
                   pl.BlockSpec((PEER_TOPK, tm), lambda i, h: (h, i)),
                   pl.BlockSpec((PEER_TOPK, tm), lambda i, h: (h, i))),
        scratch_shapes=[pltpu.VMEM((PEER_HEADS, tm, 2 * PEER_KEY_DIM), bf16)],
        compiler_params=_params("parallel", "arbitrary"),
        name="peer_route",
    )(x1, norm_w, wq_bf, keys_bf)


def _peer_body(idx_ref, gates_ref, h2_ref, x1_ref, fw_ref, u_hbm, v_hbm, o_ref, rows, sem, w_scr, *, tt):
    unroll = 8

    def row_copy(table, e, j, slot):
        return pltpu.make_async_copy(table.at[pl.ds(e, 1), :], rows.at[slot, pl.ds(j, 1), :], sem.at[slot])

    def issue(table, t, slot):
        def chunk(jj, carry):
            for r in range(unroll):
                j = jj * unroll + r
                row_copy(table, idx_ref[t, j], j, slot).start()
            return carry
        lax.fori_loop(0, PEER_SLOTS // unroll, chunk, 0)

    def wait_all(table, slot):
        pltpu.make_async_copy(table.at[pl.ds(0, PEER_SLOTS), :], rows.at[slot], sem.at[slot]).wait()

    lane = lax.broadcasted_iota(jnp.int32, (PEER_SLOTS, tt), 1)

    def gathered(table, t):
        slot = t % 2

        @pl.when(t + 1 < tt)
        def _():
            issue(table, t + 1, 1 - slot)

        wait_all(table, slot)
        return rows[slot]

    issue(u_hbm, 0, 0)

    def act_step(t, acts):
        u_rows = gathered(u_hbm, t)
        a = jnp.sum(u_rows * h2_ref[pl.ds(t, 1), :], axis=1, keepdims=True)
        return jnp.where(lane == t, a, acts)

    acts = lax.fori_loop(0, tt, act_step, jnp.zeros((PEER_SLOTS, tt), f32))
    gelu = 0.5 * acts * (1.0 + lax.erf(acts * (2.0 ** -0.5)))
    w_scr[...] = gates_ref[...] * gelu

    issue(v_hbm, 0, 0)

    def out_step(t, carry):
        v_rows = gathered(v_hbm, t)
        wcol = jnp.sum(jnp.where(lane == t, w_scr[...], 0.0), axis=1, keepdims=True)
        out = jnp.sum(v_rows * wcol, axis=0, keepdims=True)
        x2 = x1_ref[pl.ds(t, 1), :] + out
        o_ref[pl.ds(t, 1), :] = _rms(x2, fw_ref[...], NORM_EPS)
        return carry

    lax.fori_loop(0, tt, out_step, 0)


def _peer_apply(idx, gates_t, h2, x1, final_w, u, v):
    T = x1.shape[0]
    tt = PEER_TT
    assert T % tt == 0
    return pl.pallas_call(
        functools.partial(_peer_body, tt=tt),
        out_shape=jax.ShapeDtypeStruct((T, D_MODEL), f32),
        grid=(T // tt,),
        in_specs=[pl.BlockSpec((tt, PEER_SLOTS), lambda i: (i, 0), memory_space=pltpu.SMEM),
                  pl.BlockSpec((PEER_SLOTS, tt), lambda i: (0, i)),
                  pl.BlockSpec((tt, D_MODEL), lambda i: (i, 0)),
                  pl.BlockSpec((tt, D_MODEL), lambda i: (i, 0)),
                  pl.BlockSpec((1, D_MODEL), lambda i: (0, 0)),
                  pl.BlockSpec(memory_space=pl.ANY),
                  pl.BlockSpec(memory_space=pl.ANY)],
        out_specs=pl.BlockSpec((tt, D_MODEL), lambda i: (i, 0)),
        scratch_shapes=[pltpu.VMEM((2, PEER_SLOTS, D_MODEL), f32),
                        pltpu.SemaphoreType.DMA((2,)),
                        pltpu.VMEM((PEER_SLOTS, tt), f32)],
        compiler_params=_params("arbitrary"),
        name="peer_apply",
    )(idx, gates_t, h2, x1, final_w, u, v)


def _trunk(x, rel_bias, norm1_w, w_in, lambda_q1, lambda_k1, lambda_q2, lambda_k2, subln_w,
           pool_w, pool_scale, w_out, norm2_w, peer_wq, peer_keys, peer_u, peer_v, final_norm_w):
    B, S, _ = x.shape
    T = B * S
    depth = norm1_w.shape[0]
    x2d = x.reshape(T, D_MODEL)
    for l in range(depth):
        lam_init = _lambda_init(l)
        lam = (jnp.exp(jnp.sum(lambda_q1[l].astype(f32) * lambda_k1[l].astype(f32)))
               - jnp.exp(jnp.sum(lambda_q2[l].astype(f32) * lambda_k2[l].astype(f32)))
               + lam_init).reshape(1)
        proj = _inproj(x2d, norm1_w[l].reshape(1, D_MODEL), w_in[l].astype(bf16))
        att = _attention(proj.reshape(B, S, IN_PROJ_DIM), lam, rel_bias, subln_w[l].reshape(1, ATT_V_DIM),
                         1.0 - lam_init)
        x1 = _pool_outproj(x2d, att.reshape(T, ATT_WIDTH), proj, pool_w[l].astype(bf16),
                           pool_scale[l].reshape(1, POOL_WIDTH), w_out[l].astype(bf16), S)
        keys = peer_keys[l].reshape(2 * PEER_HEADS, N_KEYS, PEER_KEY_DIM).astype(bf16)
        h2, expert_t, gates_t = _peer_route(x1, norm2_w[l].reshape(1, D_MODEL), peer_wq[l].astype(bf16), keys)
        last = l == depth - 1
        fw = final_norm_w.reshape(1, D_MODEL)
        assert last, "the final norm is fused into the last layer's PEER stage"
        x2d = _peer_apply(jnp.transpose(expert_t), gates_t, h2, x1, fw, peer_u[l], peer_v[l])
    return x2d.reshape(B, S, D_MODEL)


def kernel(x_prompt, x_sample, rel_bias, norm1_w, w_in, lambda_q1, lambda_k1, lambda_q2, lambda_k2, subln_w,
           pool_w, pool_scale, w_out, norm2_w, peer_wq, peer_keys, peer_u, peer_v, final_norm_w):
    args = (rel_bias, norm1_w, w_in, lambda_q1, lambda_k1, lambda_q2, lambda_k2, subln_w,
            pool_w, pool_scale, w_out, norm2_w, peer_wq, peer_keys, peer_u, peer_v, final_norm_w)
    return (_trunk(x_prompt, *args), _trunk(x_sample, *args))
```

```python
import functools
import math

import numpy as np
import jax
import jax.numpy as jnp
from jax import lax
from jax.experimental import pallas as pl
from jax.experimental.pallas import tpu as pltpu

f32 = jnp.float32
bf16 = jnp.bfloat16
u32 = jnp.uint32

D_MODEL = 2048
ATT_WIDTH = 1024
N_ATT_HEADS = 8
ATT_HEAD_DIM = 64
ATT_V_DIM = 128
POOL_WIDTH = 1024
POOL_WINDOWS = (2, 4, 8, 16)
POOL_GROUP_DIM = 256
IN_PROJ_DIM = 3 * ATT_WIDTH + POOL_WIDTH
N_BUCKETS = 32
MAX_DISTANCE = 128
N_KEYS = 128
PEER_HEADS = 8
PEER_TOPK = 16
PEER_KEY_DIM = 128
PEER_SLOTS = PEER_HEADS * PEER_TOPK
NORM_EPS = 1e-6
SUBLN_EPS = 1e-5

LANES = 128
BF16_SUBLANES = 16
VMEM_LIMIT = 56 * 1024 * 1024
D_CHUNKS = D_MODEL // LANES

ATT_TILE = 512
INPROJ_TM, INPROJ_TN = 512, 1024
POOL_TM = 256
ROUTE_TM = 256
PEER_TT = 128
PEER_NBUF = 4
PEER_AHEAD = PEER_NBUF - 1
POOL_HALO = BF16_SUBLANES


def _lambda_init(layer_idx):
    return 0.8 - 0.6 * math.exp(-0.3 * layer_idx)


def _rms(x, w, eps):
    ms = jnp.mean(x * x, axis=-1, keepdims=True)
    return x * lax.rsqrt(ms + eps) * w


def _params(*sem):
    return pltpu.CompilerParams(dimension_semantics=sem, vmem_limit_bytes=VMEM_LIMIT)


def _inproj_body(x_ref, nw_ref, w_ref, o_ref, h_ref):
    @pl.when(pl.program_id(1) == 0)
    def _():
        h_ref[...] = _rms(x_ref[...], nw_ref[...], NORM_EPS).astype(bf16)

    o_ref[...] = jnp.dot(h_ref[...], w_ref[...], preferred_element_type=f32).astype(o_ref.dtype)


def _inproj(x2d, norm_w, w_bf):
    T, N = x2d.shape[0], w_bf.shape[1]
    tm, tn = min(INPROJ_TM, T), INPROJ_TN
    return pl.pallas_call(
        _inproj_body,
        out_shape=jax.ShapeDtypeStruct((T, N), bf16),
        grid=(T // tm, N // tn),
        in_specs=[pl.BlockSpec((tm, D_MODEL), lambda i, j: (i, 0)),
                  pl.BlockSpec((1, D_MODEL), lambda i, j: (0, 0)),
                  pl.BlockSpec((D_MODEL, tn), lambda i, j: (0, j))],
        out_specs=pl.BlockSpec((tm, tn), lambda i, j: (i, j)),
        scratch_shapes=[pltpu.VMEM((tm, D_MODEL), bf16)],
        compiler_params=_params("parallel", "arbitrary"),
        name="inproj",
    )(x2d, norm_w, w_bf)


def _bucket_of_rel(rel):
    nb = N_BUCKETS // 2
    ret = (rel > 0).astype(np.int64) * nb
    n = np.abs(rel)
    max_exact = nb // 2
    large = max_exact + (np.log(np.maximum(n, 1).astype(np.float64) / max_exact)
                         / math.log(MAX_DISTANCE / max_exact) * (nb - max_exact)).astype(np.int64)
    large = np.minimum(large, nb - 1)
    return (ret + np.where(n < max_exact, n, large)).astype(np.int32)


def _attn_body(lam_ref, cfar_ref, q_ref, k_ref, v_ref, bias_ref, sw_ref, o_ref,
               qm_ref, m_ref, l_ref, acc_ref, *, lam_scale, t, nk):
    h, qi = pl.program_id(1), pl.program_id(2)
    m_ref[...] = jnp.full(m_ref.shape, -jnp.inf, f32)
    l_ref[...] = jnp.zeros(l_ref.shape, f32)
    acc_ref[...] = jnp.zeros(acc_ref.shape, f32)

    q = q_ref[...] * jnp.asarray(ATT_HEAD_DIM ** -0.5, bf16)
    lane = lax.broadcasted_iota(jnp.int32, q.shape, 1)
    qm_ref[:t] = jnp.where(lane < ATT_HEAD_DIM, q, jnp.zeros_like(q))
    qm_ref[t:] = jnp.where(lane >= ATT_HEAD_DIM, q, jnp.zeros_like(q))
    nt = (((1,), (1,)), ((), ()))

    def chunk(j, shift_fn):
        start = pl.multiple_of(j * t, t)
        k = k_ref[pl.ds(start, t), :]
        v = v_ref[pl.ds(start, t), :]
        s = shift_fn(lax.dot_general(qm_ref[...], k, nt, preferred_element_type=f32))
        m_prev = m_ref[...]
        m_new = jnp.maximum(m_prev, jnp.max(s, axis=1, keepdims=True))
        alpha = jnp.exp(m_prev - m_new)
        p = jnp.exp(s - jnp.concatenate([m_new] * (t // LANES), axis=1))
        l_ref[...] = alpha * l_ref[...] + jnp.sum(p, axis=1, keepdims=True)
        acc_ref[...] = alpha * acc_ref[...] + jnp.dot(p.astype(bf16), v, preferred_element_type=f32)
        m_ref[...] = m_new

    def far_sweep(lo, hi, c):
        def body(j, carry):
            chunk(j, lambda s: s + c)
            return carry
        lax.fori_loop(lo, hi, body, 0)

    def add_tile(s, d):
        b = bias_ref[d + 1]
        return jnp.concatenate([s[:t] + b, s[t:] + b], axis=0)

    far_sweep(0, jnp.maximum(qi - 1, 0), cfar_ref[h, 0])
    for d in (-1, 0, 1):
        j = qi + d

        @pl.when(jnp.logical_and(j >= 0, j < nk))
        def _():
            chunk(j, lambda s: add_tile(s, d))
    far_sweep(jnp.minimum(qi + 2, nk), nk, cfar_ref[h, 1])

    lam = lam_ref[0]
    out = acc_ref[:t] / l_ref[:t] - lam * (acc_ref[t:] / l_ref[t:])
    o_ref[...] = (_rms(out, sw_ref[...], SUBLN_EPS) * lam_scale).astype(o_ref.dtype)


def _bias_tiles(rel_bias, t):
    span = 2 * t - 1
    rel_vec = np.arange(-span, span + 1)
    bias_vec = jnp.transpose(rel_bias[jnp.asarray(_bucket_of_rel(rel_vec))].astype(f32))
    tiles = []
    for d in (-1, 0, 1):
        w = lax.slice_in_dim(bias_vec, t + d * t, t + d * t + span, axis=1)
        wp = jnp.pad(w, ((0, 0), (0, 1)))
        skew = jnp.tile(wp, (1, t))[:, :t * span].reshape(-1, t, span)
        tiles.append(skew[:, :, t - 1:t - 1 + t])
    return jnp.stack(tiles, axis=1)


def _attention(proj3, lam, rel_bias, subln_w, lam_scale):
    B, S, _ = proj3.shape
    t = min(ATT_TILE, S)
    assert S % t == 0 and t >= MAX_DISTANCE
    bias_tiles = _bias_tiles(rel_bias, t)
    far = _bucket_of_rel(np.array([-MAX_DISTANCE, MAX_DISTANCE]))
    cfar = jnp.transpose(rel_bias[jnp.asarray(far)].astype(f32))
    n_heads = N_ATT_HEADS
    nk = S // t
    return pl.pallas_call(
        functools.partial(_attn_body, lam_scale=lam_scale, t=t, nk=nk),
        out_shape=jax.ShapeDtypeStruct((B, S, ATT_WIDTH), bf16),
        grid=(B, n_heads, nk),
        in_specs=[pl.BlockSpec(memory_space=pltpu.SMEM),
                  pl.BlockSpec(memory_space=pltpu.SMEM),
                  pl.BlockSpec((None, t, LANES), lambda b, h, qi: (b, qi, h)),
                  pl.BlockSpec((None, S, LANES), lambda b, h, qi: (b, 0, n_heads + h)),
                  pl.BlockSpec((None, S, LANES), lambda b, h, qi: (b, 0, 2 * n_heads + h)),
                  pl.BlockSpec((None, 3, t, t), lambda b, h, qi: (h, 0, 0, 0)),
                  pl.BlockSpec((1, ATT_V_DIM), lambda b, h, qi: (0, 0))],
        out_specs=pl.BlockSpec((None, t, LANES), lambda b, h, qi: (b, qi, h)),
        scratch_shapes=[pltpu.VMEM((2 * t, LANES), bf16),
                        pltpu.VMEM((2 * t, LANES), f32), pltpu.VMEM((2 * t, LANES), f32),
                        pltpu.VMEM((2 * t, ATT_V_DIM), f32)],
        compiler_params=_params("parallel", "parallel", "arbitrary"),
        name="diff_attention",
    )(lam, cfar, proj3, proj3, proj3, bias_tiles, subln_w)


def _pool_body(x_ref, att_ref, p_ref, prev_ref, next_ref, pw_ref, ps_ref, wo_ref, o_ref, *, seq, tm):
    i = pl.program_id(0)
    t0 = (i * tm) % seq
    prev_ok = (t0 > 0).astype(f32)
    next_ok = (t0 + tm < seq).astype(f32)
    pm = p_ref[...].astype(f32)
    ext = jnp.concatenate([prev_ref[...].astype(f32) * prev_ok, pm,
                           next_ref[...].astype(f32) * next_ok], axis=0)
    pos = t0 + lax.broadcasted_iota(jnp.int32, (tm, 1), 0)
    pooled = []
    for g, w in enumerate(POOL_WINDOWS):
        cs = slice(g * POOL_GROUP_DIM, (g + 1) * POOL_GROUP_DIM)
        arr = ext[:, cs]
        n = 1
        while n < w:
            rows = arr.shape[0]
            arr = arr[0:rows - n] + arr[n:rows]
            n *= 2
        half = w // 2
        wsum = arr[POOL_HALO - half:POOL_HALO - half + tm]
        count = (jnp.minimum(pos + half, seq) - jnp.maximum(pos - half, 0)).astype(f32)
        mixed = (wsum / count - pm[:, cs]).astype(bf16)
        pooled.append((jnp.dot(mixed, pw_ref[g], preferred_element_type=f32) * ps_ref[:, cs]).astype(bf16))
    cat = jnp.concatenate([att_ref[...]] + pooled, axis=1)
    o_ref[...] = x_ref[...] + jnp.dot(cat, wo_ref[...], preferred_element_type=f32)


def _pool_outproj(x2d, att2d, proj, pool_w_bf, pool_scale, w_out_bf, seq):
    T = x2d.shape[0]
    tm = min(POOL_TM, seq)
    assert seq % tm == 0 and tm % POOL_HALO == 0
    hb = tm // POOL_HALO
    last = T // POOL_HALO - 1
    pcol = 3 * ATT_WIDTH // POOL_WIDTH
    return pl.pallas_call(
        functools.partial(_pool_body, seq=seq, tm=tm),
        out_shape=jax.ShapeDtypeStruct((T, D_MODEL), f32),
        grid=(T // tm,),
        in_specs=[pl.BlockSpec((tm, D_MODEL), lambda i: (i, 0)),
                  pl.BlockSpec((tm, ATT_WIDTH), lambda i: (i, 0)),
                  pl.BlockSpec((tm, POOL_WIDTH), lambda i: (i, pcol)),
                  pl.BlockSpec((POOL_HALO, POOL_WIDTH), lambda i: (jnp.maximum(i * hb - 1, 0), pcol)),
                  pl.BlockSpec((POOL_HALO, POOL_WIDTH), lambda i: (jnp.minimum((i + 1) * hb, last), pcol)),
                  pl.BlockSpec((len(POOL_WINDOWS), POOL_GROUP_DIM, POOL_GROUP_DIM), lambda i: (0, 0, 0)),
                  pl.BlockSpec((1, POOL_WIDTH), lambda i: (0, 0)),
                  pl.BlockSpec((D_MODEL, D_MODEL), lambda i: (0, 0))],
        out_specs=pl.BlockSpec((tm, D_MODEL), lambda i: (i, 0)),
        compiler_params=_params("parallel"),
        name="pool_outproj",
    )(x2d, att2d, proj, proj, proj, pool_w_bf, pool_scale, w_out_bf)


def _topk_rows(s, k):
    n = s.shape[0]
    iota = lax.broadcasted_iota(jnp.int32, s.shape, 0)
    vals, idxs = [], []
    for _ in range(k):
        m = jnp.max(s, axis=0, keepdims=True)
        idx = jnp.minimum(jnp.min(jnp.where(s == m, iota, n), axis=0, keepdims=True), n - 1)
        vals.append(m)
        idxs.append(idx)
        s = jnp.where(iota == idx, -jnp.inf, s)
    return jnp.concatenate(vals, axis=0), jnp.concatenate(idxs, axis=0)


def _select_rows(table, sel):
    out = jnp.zeros(sel.shape, table.dtype)
    for a in range(table.shape[0]):
        out = jnp.where(sel == a, table[a:a + 1, :], out)
    return out


def _route_body(x1_ref, nw_ref, wq_ref, keys_ref, h2_ref, e_ref, g_ref, qp_ref):
    h = pl.program_id(1)

    @pl.when(h == 0)
    def _():
        h2 = _rms(x1_ref[...], nw_ref[...], NORM_EPS)
        h2_ref[...] = h2
        qp = jnp.dot(h2.astype(bf16), wq_ref[...], preferred_element_type=f32).astype(bf16)
        for hh in range(PEER_HEADS):
            qp_ref[hh] = qp[:, hh * 2 * PEER_KEY_DIM:(hh + 1) * 2 * PEER_KEY_DIM]

    q = qp_ref[h]
    nt = (((1,), (1,)), ((), ()))
    halves = []
    for i in range(2):
        st = lax.dot_general(keys_ref[2 * h + i], q[:, i * PEER_KEY_DIM:(i + 1) * PEER_KEY_DIM], nt,
                             preferred_element_type=f32)
        halves.append(_topk_rows(st, PEER_TOPK))
    (s1, i1), (s2, i2) = halves
    cand = jnp.concatenate([s1[a:a + 1, :] + s2 for a in range(PEER_TOPK)], axis=0)
    best, flat = _topk_rows(cand, PEER_TOPK)
    e1 = _select_rows(i1, lax.shift_right_logical(flat, int(math.log2(PEER_TOPK))))
    e2 = _select_rows(i2, lax.bitwise_and(flat, PEER_TOPK - 1))
    e_ref[...] = e1 * N_KEYS + e2
    ex = jnp.exp(best - jnp.max(best, axis=0, keepdims=True))
    g_ref[...] = ex / jnp.sum(ex, axis=0, keepdims=True)


def _peer_route(x1, norm_w, wq_bf, keys_bf):
    T = x1.shape[0]
    tm = min(ROUTE_TM, T)
    return pl.pallas_call(
        _route_body,
        out_shape=(jax.ShapeDtypeStruct((T, D_MODEL), f32),
                   jax.ShapeDtypeStruct((PEER_SLOTS, T), jnp.int32),
                   jax.ShapeDtypeStruct((PEER_SLOTS, T), f32)),
        grid=(T // tm, PEER_HEADS),
        in_specs=[pl.BlockSpec((tm, D_MODEL), lambda i, h: (i, 0)),
                  pl.BlockSpec((1, D_MODEL), lambda i, h: (0, 0)),
                  pl.BlockSpec((D_MODEL, 2 * PEER_KEY_DIM * PEER_HEADS), lambda i, h: (0, 0)),
                  pl.BlockSpec((2 * PEER_HEADS, N_KEYS, PEER_KEY_DIM), lambda i, h: (0, 0, 0))],
        out_specs=(pl.BlockSpec((tm, D_MODEL), lambda i, h: (i, 0)),
                   pl.BlockSpec((PEER_TOPK, tm), lambda i, h: (h, i)),
                   pl.BlockSpec((PEER_TOPK, tm), lambda i, h: (h, i))),
        scratch_shapes=[pltpu.VMEM((PEER_HEADS, tm, 2 * PEER_KEY_DIM), bf16)],
        compiler_params=_params("parallel", "arbitrary"),
        name="peer_route",
    )(x1, norm_w, wq_bf, keys_bf)


def _pack_uv(u, v):
    ub = lax.bitcast_convert_type(u.astype(bf16), jnp.uint16).astype(u32)
    vb = lax.bitcast_convert_type(v.astype(bf16), jnp.uint16).astype(u32)
    return ((ub << 16) | vb).reshape(u.shape[0], D_CHUNKS, LANES)


def _peer_body(idx_ref, idx_next_ref, gates_ref, h2_ref, x1_ref, fw_ref, uv_hbm, o_ref, *scratch, tt):
    bufs, sem = scratch[:PEER_NBUF], scratch[PEER_NBUF]
    i = pl.program_id(0)
    last_step = pl.num_programs(0) - 1

    def issue(iref, tn, b):
        for j in range(PEER_SLOTS):
            pltpu.make_async_copy(uv_hbm.at[iref[tn, j]], bufs[b].at[:, j, :], sem.at[b]).start()

    def wait_buf(b):
        pltpu.make_async_copy(uv_hbm.at[pl.ds(0, PEER_SLOTS)], bufs[b], sem.at[b]).wait()

    lane_t = lax.broadcasted_iota(jnp.int32, (PEER_SLOTS, tt), 1)
    hi_mask = jnp.asarray(0xFFFF0000, u32)

    def compute(t, b):
        buf = bufs[b]
        hrow = h2_ref[pl.ds(t, 1), :]
        acc = jnp.zeros((PEER_SLOTS, LANES), f32)
        for c in range(D_CHUNKS):
            u = lax.bitcast_convert_type(buf[c] & hi_mask, f32)
            acc = acc + u * hrow[:, c * LANES:(c + 1) * LANES]
        a = jnp.sum(acc, axis=1, keepdims=True)
        gate = jnp.sum(jnp.where(lane_t == t, gates_ref[...], 0.0), axis=1, keepdims=True)
        w = gate * (0.5 * a * (1.0 + lax.erf(a * (2.0 ** -0.5))))
        wb = jnp.broadcast_to(w, (PEER_SLOTS, LANES))
        outs = []
        for c in range(D_CHUNKS):
            v = lax.bitcast_convert_type(buf[c] << 16, f32)
            outs.append(jnp.sum(v * wb, axis=0, keepdims=True))
        x2 = x1_ref[pl.ds(t, 1), :] + jnp.concatenate(outs, axis=1)
        o_ref[pl.ds(t, 1), :] = _rms(x2, fw_ref[...], NORM_EPS)

    def step(t, b, iref, tn):
        wait_buf(b)
        issue(iref, tn, (b + PEER_AHEAD) % PEER_NBUF)
        compute(t, b)

    @pl.when(i == 0)
    def _():
        for t in range(PEER_AHEAD):
            issue(idx_ref, t, t)

    groups = tt // PEER_NBUF

    def group(g, carry):
        for b in range(PEER_NBUF):
            t = g * PEER_NBUF + b
            step(t, b, idx_ref, t + PEER_AHEAD)
        return carry

    lax.fori_loop(0, groups - 1, group, 0)
    for b in range(PEER_NBUF):
        t = (groups - 1) * PEER_NBUF + b
        tn = t + PEER_AHEAD
        if tn < tt:
            step(t, b, idx_ref, tn)
        else:
            step(t, b, idx_next_ref, tn - tt)

    @pl.when(i == last_step)
    def _():
        for t in range(PEER_AHEAD):
            wait_buf(t)


def _peer_apply(idx, gates_t, h2, x1, final_w, uv):
    T = x1.shape[0]
    tt = PEER_TT
    assert T % tt == 0 and tt % PEER_NBUF == 0 and tt >= 2 * PEER_NBUF
    nsteps = T // tt
    return pl.pallas_call(
        functools.partial(_peer_body, tt=tt),
        out_shape=jax.ShapeDtypeStruct((T, D_MODEL), f32),
        grid=(nsteps,),
        in_specs=[pl.BlockSpec((tt, PEER_SLOTS), lambda i: (i, 0), memory_space=pltpu.SMEM),
                  pl.BlockSpec((tt, PEER_SLOTS), lambda i: (jnp.minimum(i + 1, nsteps - 1), 0),
                               memory_space=pltpu.SMEM),
                  pl.BlockSpec((PEER_SLOTS, tt), lambda i: (0, i)),
                  pl.BlockSpec((tt, D_MODEL), lambda i: (i, 0)),
                  pl.BlockSpec((tt, D_MODEL), lambda i: (i, 0)),
                  pl.BlockSpec((1, D_MODEL), lambda i: (0, 0)),
                  pl.BlockSpec(memory_space=pl.ANY)],
        out_specs=pl.BlockSpec((tt, D_MODEL), lambda i: (i, 0)),
        scratch_shapes=[pltpu.VMEM((D_CHUNKS, PEER_SLOTS, LANES), u32) for _ in range(PEER_NBUF)]
        + [pltpu.SemaphoreType.DMA((PEER_NBUF,))],
        compiler_params=_params("arbitrary"),
        name="peer_apply",
    )(idx, idx, gates_t, h2, x1, final_w, uv)


def _trunk(x, rel_bias, norm1_w, w_in, lambda_q1, lambda_k1, lambda_q2, lambda_k2, subln_w,
           pool_w, pool_scale, w_out, norm2_w, peer_wq, peer_keys, peer_u, peer_v, final_norm_w):
    B, S, _ = x.shape
    T = B * S
    depth = norm1_w.shape[0]
    assert depth == 1, "the final norm is fused into the (single) layer's PEER stage"
    l = 0
    x2d = x.reshape(T, D_MODEL)
    lam_init = _lambda_init(l)
    lam = (jnp.exp(jnp.sum(lambda_q1[l].astype(f32) * lambda_k1[l].astype(f32)))
           - jnp.exp(jnp.sum(lambda_q2[l].astype(f32) * lambda_k2[l].astype(f32)))
           + lam_init).reshape(1)
    proj = _inproj(x2d, norm1_w[l].reshape(1, D_MODEL), w_in[l].astype(bf16))
    att = _attention(proj.reshape(B, S, IN_PROJ_DIM), lam, rel_bias, subln_w[l].reshape(1, ATT_V_DIM),
                     1.0 - lam_init)
    x1 = _pool_outproj(x2d, att.reshape(T, ATT_WIDTH), proj, pool_w[l].astype(bf16),
                       pool_scale[l].reshape(1, POOL_WIDTH), w_out[l].astype(bf16), S)
    keys = peer_keys[l].reshape(2 * PEER_HEADS, N_KEYS, PEER_KEY_DIM).astype(bf16)
    h2, expert_t, gates_t = _peer_route(x1, norm2_w[l].reshape(1, D_MODEL), peer_wq[l].astype(bf16), keys)
    y = _peer_apply(jnp.transpose(expert_t), gates_t, h2, x1, final_norm_w.reshape(1, D_MODEL),
                    _pack_uv(peer_u[l], peer_v[l]))
    return y.reshape(B, S, D_MODEL)


def kernel(x_prompt, x_sample, rel_bias, norm1_w, w_in, lambda_q1, lambda_k1, lambda_q2, lambda_k2, subln_w,
           pool_w, pool_scale, w_out, norm2_w, peer_wq, peer_keys, peer_u, peer_v, final_norm_w):
    args = (rel_bias, norm1_w, w_in, lambda_q1, lambda_k1, lambda_q2, lambda_k2, subln_w,
            pool_w, pool_scale, w_out, norm2_w, peer_wq, peer_keys, peer_u, peer_v, final_norm_w)
    return (_trunk(x_prompt, *args), _trunk(x_sample, *args))
```

```python
import functools
import math

import numpy as np
import jax
import jax.numpy as jnp
from jax import lax
from jax.experimental import pallas as pl
from jax.experimental.pallas import tpu as pltpu

f32 = jnp.float32
bf16 = jnp.bfloat16
u32 = jnp.uint32

D_MODEL = 2048
ATT_WIDTH = 1024
N_ATT_HEADS = 8
ATT_HEAD_DIM = 64
ATT_V_DIM = 128
POOL_WIDTH = 1024
POOL_WINDOWS = (2, 4, 8, 16)
POOL_GROUP_DIM = 256
IN_PROJ_DIM = 3 * ATT_WIDTH + POOL_WIDTH
N_BUCKETS = 32
MAX_DISTANCE = 128
N_KEYS = 128
PEER_HEADS = 8
PEER_TOPK = 16
PEER_KEY_DIM = 128
PEER_SLOTS = PEER_HEADS * PEER_TOPK
NORM_EPS = 1e-6
SUBLN_EPS = 1e-5
LOG2E = math.log2(math.e)

LANES = 128
BF16_SUBLANES = 16
VMEM_LIMIT = 56 * 1024 * 1024
D_CHUNKS = D_MODEL // LANES

ATT_TILE = 512
INPROJ_TM, INPROJ_TN = 512, 1024
POOL_TM = 256
ROUTE_TM = 256
PEER_TT = 128
PEER_NBUF = 4
PEER_AHEAD = PEER_NBUF - 1
POOL_HALO = BF16_SUBLANES


def _lambda_init(layer_idx):
    return 0.8 - 0.6 * math.exp(-0.3 * layer_idx)


def _rms(x, w, eps):
    ms = jnp.mean(x * x, axis=-1, keepdims=True)
    return x * lax.rsqrt(ms + eps) * w


def _params(*sem):
    return pltpu.CompilerParams(dimension_semantics=sem, vmem_limit_bytes=VMEM_LIMIT)


def _inproj_body(x_ref, nw_ref, w_ref, o_ref, h_ref):
    @pl.when(pl.program_id(1) == 0)
    def _():
        h_ref[...] = _rms(x_ref[...], nw_ref[...], NORM_EPS).astype(bf16)

    o_ref[...] = jnp.dot(h_ref[...], w_ref[...], preferred_element_type=f32).astype(o_ref.dtype)


def _inproj(x2d, norm_w, w_bf):
    T, N = x2d.shape[0], w_bf.shape[1]
    tm, tn = min(INPROJ_TM, T), INPROJ_TN
    return pl.pallas_call(
        _inproj_body,
        out_shape=jax.ShapeDtypeStruct((T, N), bf16),
        grid=(T // tm, N // tn),
        in_specs=[pl.BlockSpec((tm, D_MODEL), lambda i, j: (i, 0)),
                  pl.BlockSpec((1, D_MODEL), lambda i, j: (0, 0)),
                  pl.BlockSpec((D_MODEL, tn), lambda i, j: (0, j))],
        out_specs=pl.BlockSpec((tm, tn), lambda i, j: (i, j)),
        scratch_shapes=[pltpu.VMEM((tm, D_MODEL), bf16)],
        compiler_params=_params("parallel", "arbitrary"),
        name="inproj",
    )(x2d, norm_w, w_bf)


def _bucket_of_rel(rel):
    nb = N_BUCKETS // 2
    ret = (rel > 0).astype(np.int64) * nb
    n = np.abs(rel)
    max_exact = nb // 2
    large = max_exact + (np.log(np.maximum(n, 1).astype(np.float64) / max_exact)
                         / math.log(MAX_DISTANCE / max_exact) * (nb - max_exact)).astype(np.int64)
    large = np.minimum(large, nb - 1)
    return (ret + np.where(n < max_exact, n, large)).astype(np.int32)


def _attn_body(lam_ref, cfar_ref, q_ref, k_ref, v_ref, bias_ref, sw_ref, o_ref,
               qm_ref, m_ref, l_ref, acc_ref, *, lam_scale, t, nk):
    h, qi = pl.program_id(1), pl.program_id(2)
    m_ref[...] = jnp.full(m_ref.shape, -jnp.inf, f32)
    l_ref[...] = jnp.zeros(l_ref.shape, f32)
    acc_ref[...] = jnp.zeros(acc_ref.shape, f32)

    q = (q_ref[...].astype(f32) * (LOG2E * ATT_HEAD_DIM ** -0.5)).astype(bf16)
    lane = lax.broadcasted_iota(jnp.int32, q.shape, 1)
    qm_ref[:t] = jnp.where(lane < ATT_HEAD_DIM, q, jnp.zeros_like(q))
    qm_ref[t:] = jnp.where(lane >= ATT_HEAD_DIM, q, jnp.zeros_like(q))
    nt = (((1,), (1,)), ((), ()))

    def chunk(j, tile_d, c):
        start = pl.multiple_of(j * t, t)
        k = k_ref[pl.ds(start, t), :]
        v = v_ref[pl.ds(start, t), :]
        s = lax.dot_general(qm_ref[...], k, nt, preferred_element_type=f32)
        m_prev = m_ref[...]
        if tile_d is None:
            m_new = jnp.maximum(m_prev, jnp.max(s, axis=1, keepdims=True) + c)
            shift = m_new - c
        else:
            b = bias_ref[tile_d + 1]
            s = jnp.concatenate([s[:t] + b, s[t:] + b], axis=0)
            m_new = jnp.maximum(m_prev, jnp.max(s, axis=1, keepdims=True))
            shift = m_new
        alpha = jnp.exp2(m_prev - m_new)
        p = jnp.exp2(s - jnp.concatenate([shift] * (t // LANES), axis=1))
        l_ref[...] = alpha * l_ref[...] + jnp.sum(p, axis=1, keepdims=True)
        acc_ref[...] = alpha * acc_ref[...] + jnp.dot(p.astype(bf16), v, preferred_element_type=f32)
        m_ref[...] = m_new

    def far_sweep(lo, hi, c):
        def body(j, carry):
            chunk(j, None, c)
            return carry
        lax.fori_loop(lo, hi, body, 0)

    far_sweep(0, jnp.maximum(qi - 1, 0), cfar_ref[h, 0])
    for d in (-1, 0, 1):
        j = qi + d

        @pl.when(jnp.logical_and(j >= 0, j < nk))
        def _():
            chunk(j, d, None)
    far_sweep(jnp.minimum(qi + 2, nk), nk, cfar_ref[h, 1])

    lam = lam_ref[0]
    out = acc_ref[:t] / l_ref[:t] - lam * (acc_ref[t:] / l_ref[t:])
    o_ref[...] = (_rms(out, sw_ref[...], SUBLN_EPS) * lam_scale).astype(o_ref.dtype)


def _bias_tiles(rel_bias, t):
    span = 2 * t - 1
    rel_vec = np.arange(-span, span + 1)
    bias_vec = jnp.transpose(rel_bias[jnp.asarray(_bucket_of_rel(rel_vec))].astype(f32))
    tiles = []
    for d in (-1, 0, 1):
        w = lax.slice_in_dim(bias_vec, t + d * t, t + d * t + span, axis=1)
        wp = jnp.pad(w, ((0, 0), (0, 1)))
        skew = jnp.tile(wp, (1, t))[:, :t * span].reshape(-1, t, span)
        tiles.append(skew[:, :, t - 1:t - 1 + t])
    return jnp.stack(tiles, axis=1)


def _attention(proj3, lam, rel_bias, subln_w, lam_scale):
    B, S, _ = proj3.shape
    t = min(ATT_TILE, S)
    assert S % t == 0 and t >= MAX_DISTANCE
    bias_tiles = _bias_tiles(rel_bias, t) * LOG2E
    far = _bucket_of_rel(np.array([-MAX_DISTANCE, MAX_DISTANCE]))
    cfar = jnp.transpose(rel_bias[jnp.asarray(far)].astype(f32)) * LOG2E
    n_heads = N_ATT_HEADS
    nk = S // t
    return pl.pallas_call(
        functools.partial(_attn_body, lam_scale=lam_scale, t=t, nk=nk),
        out_shape=jax.ShapeDtypeStruct((B, S, ATT_WIDTH), bf16),
        grid=(B, n_heads, nk),
        in_specs=[pl.BlockSpec(memory_space=pltpu.SMEM),
                  pl.BlockSpec(memory_space=pltpu.SMEM),
                  pl.BlockSpec((None, t, LANES), lambda b, h, qi: (b, qi, h)),
                  pl.BlockSpec((None, S, LANES), lambda b, h, qi: (b, 0, n_heads + h)),
                  pl.BlockSpec((None, S, LANES), lambda b, h, qi: (b, 0, 2 * n_heads + h)),
                  pl.BlockSpec((None, 3, t, t), lambda b, h, qi: (h, 0, 0, 0)),
                  pl.BlockSpec((1, ATT_V_DIM), lambda b, h, qi: (0, 0))],
        out_specs=pl.BlockSpec((None, t, LANES), lambda b, h, qi: (b, qi, h)),
        scratch_shapes=[pltpu.VMEM((2 * t, LANES), bf16),
                        pltpu.VMEM((2 * t, LANES), f32), pltpu.VMEM((2 * t, LANES), f32),
                        pltpu.VMEM((2 * t, ATT_V_DIM), f32)],
        compiler_params=_params("parallel", "parallel", "arbitrary"),
        name="diff_attention",
    )(lam, cfar, proj3, proj3, proj3, bias_tiles, subln_w)


def _pool_body(x_ref, att_ref, p_ref, prev_ref, next_ref, pw_ref, ps_ref, wo_ref, o_ref, *, seq, tm):
    i = pl.program_id(0)
    t0 = (i * tm) % seq
    prev_ok = (t0 > 0).astype(f32)
    next_ok = (t0 + tm < seq).astype(f32)
    pm = p_ref[...].astype(f32)
    ext = jnp.concatenate([prev_ref[...].astype(f32) * prev_ok, pm,
                           next_ref[...].astype(f32) * next_ok], axis=0)
    pos = t0 + lax.broadcasted_iota(jnp.int32, (tm, 1), 0)
    pooled = []
    for g, w in enumerate(POOL_WINDOWS):
        cs = slice(g * POOL_GROUP_DIM, (g + 1) * POOL_GROUP_DIM)
        arr = ext[:, cs]
        n = 1
        while n < w:
            rows = arr.shape[0]
            arr = arr[0:rows - n] + arr[n:rows]
            n *= 2
        half = w // 2
        wsum = arr[POOL_HALO - half:POOL_HALO - half + tm]
        count = (jnp.minimum(pos + half, seq) - jnp.maximum(pos - half, 0)).astype(f32)
        mixed = (wsum / count - pm[:, cs]).astype(bf16)
        pooled.append((jnp.dot(mixed, pw_ref[g], preferred_element_type=f32) * ps_ref[:, cs]).astype(bf16))
    cat = jnp.concatenate([att_ref[...]] + pooled, axis=1)
    o_ref[...] = x_ref[...] + jnp.dot(cat, wo_ref[...], preferred_element_type=f32)


def _pool_outproj(x2d, att2d, proj, pool_w_bf, pool_scale, w_out_bf, seq):
    T = x2d.shape[0]
    tm = min(POOL_TM, seq)
    assert seq % tm == 0 and tm % POOL_HALO == 0
    hb = tm // POOL_HALO
    last = T // POOL_HALO - 1
    pcol = 3 * ATT_WIDTH // POOL_WIDTH
    return pl.pallas_call(
        functools.partial(_pool_body, seq=seq, tm=tm),
        out_shape=jax.ShapeDtypeStruct((T, D_MODEL), f32),
        grid=(T // tm,),
        in_specs=[pl.BlockSpec((tm, D_MODEL), lambda i: (i, 0)),
                  pl.BlockSpec((tm, ATT_WIDTH), lambda i: (i, 0)),
                  pl.BlockSpec((tm, POOL_WIDTH), lambda i: (i, pcol)),
                  pl.BlockSpec((POOL_HALO, POOL_WIDTH), lambda i: (jnp.maximum(i * hb - 1, 0), pcol)),
                  pl.BlockSpec((POOL_HALO, POOL_WIDTH), lambda i: (jnp.minimum((i + 1) * hb, last), pcol)),
                  pl.BlockSpec((len(POOL_WINDOWS), POOL_GROUP_DIM, POOL_GROUP_DIM), lambda i: (0, 0, 0)),
                  pl.BlockSpec((1, POOL_WIDTH), lambda i: (0, 0)),
                  pl.BlockSpec((D_MODEL, D_MODEL), lambda i: (0, 0))],
        out_specs=pl.BlockSpec((tm, D_MODEL), lambda i: (i, 0)),
        compiler_params=_params("parallel"),
        name="pool_outproj",
    )(x2d, att2d, proj, proj, proj, pool_w_bf, pool_scale, w_out_bf)


def _topk_rows(s, k, ids=None, id_bound=None):
    if ids is None:
        ids = lax.broadcasted_iota(jnp.int32, s.shape, 0)
        id_bound = s.shape[0]
    vals, picked = [], []
    for _ in range(k):
        m = jnp.max(s, axis=0, keepdims=True)
        pick = jnp.minimum(jnp.min(jnp.where(s == m, ids, id_bound), axis=0, keepdims=True), id_bound - 1)
        vals.append(m)
        picked.append(pick)
        s = jnp.where(ids == pick, -jnp.inf, s)
    return jnp.concatenate(vals, axis=0), jnp.concatenate(picked, axis=0)


def _candidate_grid(s1, s2):
    k, hk = PEER_TOPK, PEER_TOPK // 2
    m = s1.shape[1]
    iota_k = lax.broadcasted_iota(jnp.int32, (k, m), 0)
    iota_h = lax.broadcasted_iota(jnp.int32, (hk, m), 0)
    vals = [s1[0:1, :] + s2]
    ids = [iota_k]
    for a in range(1, hk):
        vals.append(s1[a:a + 1, :] + s2[0:hk, :])
        ids.append(iota_h + a * k)
    vals.append(s1[hk:k, :] + s2[0:1, :])
    ids.append((iota_h + hk) * k)
    return jnp.concatenate(vals, axis=0), jnp.concatenate(ids, axis=0)


def _select_rows(table, sel):
    out = jnp.zeros(sel.shape, table.dtype)
    for a in range(table.shape[0]):
        out = jnp.where(sel == a, table[a:a + 1, :], out)
    return out


def _route_body(x1_ref, nw_ref, wq_ref, keys_ref, h2_ref, e_ref, g_ref, qp_ref):
    h = pl.program_id(1)

    @pl.when(h == 0)
    def _():
        h2 = _rms(x1_ref[...], nw_ref[...], NORM_EPS)
        h2_ref[...] = h2
        qp = jnp.dot(h2.astype(bf16), wq_ref[...], preferred_element_type=f32).astype(bf16)
        for hh in range(PEER_HEADS):
            qp_ref[hh] = qp[:, hh * 2 * PEER_KEY_DIM:(hh + 1) * 2 * PEER_KEY_DIM]

    nt = (((1,), (1,)), ((), ()))
    tm = qp_ref.shape[1]
    for c0 in range(0, tm, LANES):
        cols = slice(c0, c0 + LANES)
        q = qp_ref[h, cols, :]
        halves = []
        for i in range(2):
            st = lax.dot_general(keys_ref[2 * h + i], q[:, i * PEER_KEY_DIM:(i + 1) * PEER_KEY_DIM], nt,
                                 preferred_element_type=f32)
            halves.append(_topk_rows(st, PEER_TOPK))
        (s1, i1), (s2, i2) = halves
        cand, flat_ids = _candidate_grid(s1, s2)
        best, flat = _topk_rows(cand, PEER_TOPK, flat_ids, PEER_TOPK * PEER_TOPK)
        e1 = _select_rows(i1, lax.shift_right_logical(flat, int(math.log2(PEER_TOPK))))
        e2 = _select_rows(i2, lax.bitwise_and(flat, PEER_TOPK - 1))
        e_ref[:, cols] = e1 * N_KEYS + e2
        ex = jnp.exp(best - jnp.max(best, axis=0, keepdims=True))
        g_ref[:, cols] = ex / jnp.sum(ex, axis=0, keepdims=True)


def _peer_route(x1, norm_w, wq_bf, keys_bf):
    T = x1.shape[0]
    tm = min(ROUTE_TM, T)
    return pl.pallas_call(
        _route_body,
        out_shape=(jax.ShapeDtypeStruct((T, D_MODEL), f32),
                   jax.ShapeDtypeStruct((PEER_SLOTS, T), jnp.int32),
                   jax.ShapeDtypeStruct((PEER_SLOTS, T), f32)),
        grid=(T // tm, PEER_HEADS),
        in_specs=[pl.BlockSpec((tm, D_MODEL), lambda i, h: (i, 0)),
                  pl.BlockSpec((1, D_MODEL), lambda i, h: (0, 0)),
                  pl.BlockSpec((D_MODEL, 2 * PEER_KEY_DIM * PEER_HEADS), lambda i, h: (0, 0)),
                  pl.BlockSpec((2 * PEER_HEADS, N_KEYS, PEER_KEY_DIM), lambda i, h: (0, 0, 0))],
        out_specs=(pl.BlockSpec((tm, D_MODEL), lambda i, h: (i, 0)),
                   pl.BlockSpec((PEER_TOPK, tm), lambda i, h: (h, i)),
                   pl.BlockSpec((PEER_TOPK, tm), lambda i, h: (h, i))),
        scratch_shapes=[pltpu.VMEM((PEER_HEADS, tm, 2 * PEER_KEY_DIM), bf16)],
        compiler_params=_params("parallel", "arbitrary"),
        name="peer_route",
    )(x1, norm_w, wq_bf, keys_bf)


def _pack_uv(u, v):
    ub = lax.bitcast_convert_type(u.astype(bf16), jnp.uint16).astype(u32)
    vb = lax.bitcast_convert_type(v.astype(bf16), jnp.uint16).astype(u32)
    return ((ub << 16) | vb).reshape(u.shape[0], D_CHUNKS, LANES)


def _peer_body(idx_ref, idx_next_ref, gates_ref, h2_ref, x1_ref, fw_ref, uv_hbm, o_ref, *scratch, tt):
    bufs, (sem, w_ref, row_ref) = scratch[:PEER_NBUF], scratch[PEER_NBUF:]
    i = pl.program_id(0)
    last_step = pl.num_programs(0) - 1

    def issue(iref, tn, b, lo=0, hi=PEER_SLOTS):
        for j in range(lo, hi):
            pltpu.make_async_copy(uv_hbm.at[iref[tn, j]], bufs[b].at[:, j, :], sem.at[b]).start()

    def wait_buf(b):
        pltpu.make_async_copy(uv_hbm.at[pl.ds(0, PEER_SLOTS)], bufs[b], sem.at[b]).wait()

    lane_t = lax.broadcasted_iota(jnp.int32, (PEER_SLOTS, tt), 1)
    hi_mask = jnp.asarray(0xFFFF0000, u32)

    per_chunk = PEER_SLOTS // D_CHUNKS
    chunk = lambda c: slice(c * LANES, (c + 1) * LANES)

    def act_chunk(b, hrow, c, acc):
        u = lax.bitcast_convert_type(bufs[b][c] & hi_mask, f32)
        return acc + u * hrow[:, chunk(c)]

    def finish_weights(t, acc):
        a = jnp.sum(acc, axis=1, keepdims=True)
        gate = jnp.sum(jnp.where(lane_t == t, gates_ref[...], 0.0), axis=1, keepdims=True)
        w = gate * (0.5 * a * (1.0 + lax.erf(a * (2.0 ** -0.5))))
        w_ref[...] = jnp.broadcast_to(w, (PEER_SLOTS, LANES))

    def step(t, b, iref, tn, has_next):
        b1 = (b + 1) % PEER_NBUF
        nb = (b + PEER_AHEAD) % PEER_NBUF
        if has_next:
            wait_buf(b1)
            hrow1 = h2_ref[pl.ds(t + 1, 1), :]
        wb = w_ref[...]
        xrow = x1_ref[pl.ds(t, 1), :]
        acc = jnp.zeros((PEER_SLOTS, LANES), f32)
        for c in range(D_CHUNKS):
            issue(iref, tn, nb, c * per_chunk, (c + 1) * per_chunk)
            if has_next:
                acc = act_chunk(b1, hrow1, c, acc)
            v = lax.bitcast_convert_type(bufs[b][c] << 16, f32)
            row_ref[:, chunk(c)] = xrow[:, chunk(c)] + jnp.sum(v * wb, axis=0, keepdims=True)
        if has_next:
            finish_weights(t + 1, acc)
        o_ref[pl.ds(t, 1), :] = _rms(row_ref[...], fw_ref[...], NORM_EPS)

    @pl.when(i == 0)
    def _():
        for t in range(PEER_AHEAD):
            issue(idx_ref, t, t)

    wait_buf(0)
    hrow0 = h2_ref[pl.ds(0, 1), :]
    acc0 = jnp.zeros((PEER_SLOTS, LANES), f32)
    for c in range(D_CHUNKS):
        acc0 = act_chunk(0, hrow0, c, acc0)
    finish_weights(0, acc0)

    groups = tt // PEER_NBUF

    def group(g, carry):
        for b in range(PEER_NBUF):
            t = g * PEER_NBUF + b
            step(t, b, idx_ref, t + PEER_AHEAD, True)
        return carry

    lax.fori_loop(0, groups - 1, group, 0)
    for b in range(PEER_NBUF):
        t = (groups - 1) * PEER_NBUF + b
        tn = t + PEER_AHEAD
        if tn < tt:
            step(t, b, idx_ref, tn, True)
        else:
            step(t, b, idx_next_ref, tn - tt, t + 1 < tt)

    @pl.when(i == last_step)
    def _():
        for t in range(PEER_AHEAD):
            wait_buf(t)


def _peer_apply(idx, gates_t, h2, x1, final_w, uv):
    T = x1.shape[0]
    tt = PEER_TT
    assert T % tt == 0 and tt % PEER_NBUF == 0 and tt >= 2 * PEER_NBUF
    nsteps = T // tt
    return pl.pallas_call(
        functools.partial(_peer_body, tt=tt),
        out_shape=jax.ShapeDtypeStruct((T, D_MODEL), f32),
        grid=(nsteps,),
        in_specs=[pl.BlockSpec((tt, PEER_SLOTS), lambda i: (i, 0), memory_space=pltpu.SMEM),
                  pl.BlockSpec((tt, PEER_SLOTS), lambda i: (jnp.minimum(i + 1, nsteps - 1), 0),
                               memory_space=pltpu.SMEM),
                  pl.BlockSpec((PEER_SLOTS, tt), lambda i: (0, i)),
                  pl.BlockSpec((tt, D_MODEL), lambda i: (i, 0)),
                  pl.BlockSpec((tt, D_MODEL), lambda i: (i, 0)),
                  pl.BlockSpec((1, D_MODEL), lambda i: (0, 0)),
                  pl.BlockSpec(memory_space=pl.ANY)],
        out_specs=pl.BlockSpec((tt, D_MODEL), lambda i: (i, 0)),
        scratch_shapes=[pltpu.VMEM((D_CHUNKS, PEER_SLOTS, LANES), u32) for _ in range(PEER_NBUF)]
        + [pltpu.SemaphoreType.DMA((PEER_NBUF,)),
           pltpu.VMEM((PEER_SLOTS, LANES), f32),
           pltpu.VMEM((1, D_MODEL), f32)],
        compiler_params=_params("arbitrary"),
        name="peer_apply",
    )(idx, idx, gates_t, h2, x1, final_w, uv)


def _trunk(x, rel_bias, norm1_w, w_in, lambda_q1, lambda_k1, lambda_q2, lambda_k2, subln_w,
           pool_w, pool_scale, w_out, norm2_w, peer_wq, peer_keys, peer_u, peer_v, final_norm_w):
    B, S, _ = x.shape
    T = B * S
    depth = norm1_w.shape[0]
    assert depth == 1, "the final norm is fused into the (single) layer's PEER stage"
    l = 0
    x2d = x.reshape(T, D_MODEL)
    lam_init = _lambda_init(l)
    lam = (jnp.exp(jnp.sum(lambda_q1[l].astype(f32) * lambda_k1[l].astype(f32)))
           - jnp.exp(jnp.sum(lambda_q2[l].astype(f32) * lambda_k2[l].astype(f32)))
           + lam_init).reshape(1)
    proj = _inproj(x2d, norm1_w[l].reshape(1, D_MODEL), w_in[l].astype(bf16))
    att = _attention(proj.reshape(B, S, IN_PROJ_DIM), lam, rel_bias, subln_w[l].reshape(1, ATT_V_DIM),
                     1.0 - lam_init)
    x1 = _pool_outproj(x2d, att.reshape(T, ATT_WIDTH), proj, pool_w[l].astype(bf16),
                       pool_scale[l].reshape(1, POOL_WIDTH), w_out[l].astype(bf16), S)
    keys = peer_keys[l].reshape(2 * PEER_HEADS, N_KEYS, PEER_KEY_DIM).astype(bf16)
    h2, expert_t, gates_t = _peer_route(x1, norm2_w[l].reshape(1, D_MODEL), peer_wq[l].astype(bf16), keys)
    y = _peer_apply(jnp.transpose(expert_t), gates_t, h2, x1, final_norm_w.reshape(1, D_MODEL),
                    _pack_uv(peer_u[l], peer_v[l]))
    return y.reshape(B, S, D_MODEL)


def kernel(x_prompt, x_sample, rel_bias, norm1_w, w_in, lambda_q1, lambda_k1, lambda_q2, lambda_k2, subln_w,
           pool_w, pool_scale, w_out, norm2_w, peer_wq, peer_keys, peer_u, peer_v, final_norm_w):
    args = (rel_bias, norm1_w, w_in, lambda_q1, lambda_k1, lambda_q2, lambda_k2, subln_w,
            pool_w, pool_scale, w_out, norm2_w, peer_wq, peer_keys, peer_u, peer_v, final_norm_w)
    return (_trunk(x_prompt, *args), _trunk(x_sample, *args))
```

```python
import functools
import math

import numpy as np
import jax
import jax.numpy as jnp
from jax import lax
from jax.experimental import pallas as pl
from jax.experimental.pallas import tpu as pltpu

f32 = jnp.float32
bf16 = jnp.bfloat16
u32 = jnp.uint32

D_MODEL = 2048
ATT_WIDTH = 1024
N_ATT_HEADS = 8
ATT_HEAD_DIM = 64
ATT_V_DIM = 128
POOL_WIDTH = 1024
POOL_WINDOWS = (2, 4, 8, 16)
POOL_GROUP_DIM = 256
IN_PROJ_DIM = 3 * ATT_WIDTH + POOL_WIDTH
N_BUCKETS = 32
MAX_DISTANCE = 128
N_KEYS = 128
PEER_HEADS = 8
PEER_TOPK = 16
PEER_KEY_DIM = 128
PEER_SLOTS = PEER_HEADS * PEER_TOPK
NORM_EPS = 1e-6
SUBLN_EPS = 1e-5
LOG2E = math.log2(math.e)

LANES = 128
BF16_SUBLANES = 16
VMEM_LIMIT = 56 * 1024 * 1024
D_CHUNKS = D_MODEL // LANES

ATT_TILE = 512
INPROJ_TM, INPROJ_TN = 512, 1024
POOL_TM = 256
ROUTE_TM = 256
PEER_TT = 128
PEER_NBUF = 8
PEER_AHEAD = PEER_NBUF - 1
POOL_HALO = BF16_SUBLANES


def _lambda_init(layer_idx):
    return 0.8 - 0.6 * math.exp(-0.3 * layer_idx)


def _rms(x, w, eps):
    ms = jnp.mean(x * x, axis=-1, keepdims=True)
    return x * lax.rsqrt(ms + eps) * w


def _params(*sem):
    return pltpu.CompilerParams(dimension_semantics=sem, vmem_limit_bytes=VMEM_LIMIT)


def _inproj_body(x_ref, nw_ref, w_ref, o_ref, h_ref):
    @pl.when(pl.program_id(1) == 0)
    def _():
        h_ref[...] = _rms(x_ref[...], nw_ref[...], NORM_EPS).astype(bf16)

    o_ref[...] = jnp.dot(h_ref[...], w_ref[...], preferred_element_type=f32).astype(o_ref.dtype)


def _inproj(x2d, norm_w, w_bf):
    T, N = x2d.shape[0], w_bf.shape[1]
    tm, tn = min(INPROJ_TM, T), INPROJ_TN
    return pl.pallas_call(
        _inproj_body,
        out_shape=jax.ShapeDtypeStruct((T, N), bf16),
        grid=(T // tm, N // tn),
        in_specs=[pl.BlockSpec((tm, D_MODEL), lambda i, j: (i, 0)),
                  pl.BlockSpec((1, D_MODEL), lambda i, j: (0, 0)),
                  pl.BlockSpec((D_MODEL, tn), lambda i, j: (0, j))],
        out_specs=pl.BlockSpec((tm, tn), lambda i, j: (i, j)),
        scratch_shapes=[pltpu.VMEM((tm, D_MODEL), bf16)],
        compiler_params=_params("parallel", "arbitrary"),
        name="inproj",
    )(x2d, norm_w, w_bf)


def _bucket_of_rel(rel):
    nb = N_BUCKETS // 2
    ret = (rel > 0).astype(np.int64) * nb
    n = np.abs(rel)
    max_exact = nb // 2
    large = max_exact + (np.log(np.maximum(n, 1).astype(np.float64) / max_exact)
                         / math.log(MAX_DISTANCE / max_exact) * (nb - max_exact)).astype(np.int64)
    large = np.minimum(large, nb - 1)
    return (ret + np.where(n < max_exact, n, large)).astype(np.int32)


def _attn_body(lam_ref, cfar_ref, q_ref, k_ref, v_ref, bias_ref, sw_ref, o_ref,
               qm_ref, m_ref, l_ref, acc_ref, *, lam_scale, t, nk):
    h, qi = pl.program_id(1), pl.program_id(2)
    m_ref[...] = jnp.full(m_ref.shape, -jnp.inf, f32)
    l_ref[...] = jnp.zeros(l_ref.shape, f32)
    acc_ref[...] = jnp.zeros(acc_ref.shape, f32)

    q = (q_ref[...].astype(f32) * (LOG2E * ATT_HEAD_DIM ** -0.5)).astype(bf16)
    lane = lax.broadcasted_iota(jnp.int32, q.shape, 1)
    qm_ref[:t] = jnp.where(lane < ATT_HEAD_DIM, q, jnp.zeros_like(q))
    qm_ref[t:] = jnp.where(lane >= ATT_HEAD_DIM, q, jnp.zeros_like(q))
    nt = (((1,), (1,)), ((), ()))

    def chunk(j, tile_d, c):
        start = pl.multiple_of(j * t, t)
        k = k_ref[pl.ds(start, t), :]
        v = v_ref[pl.ds(start, t), :]
        s = lax.dot_general(qm_ref[...], k, nt, preferred_element_type=f32)
        m_prev = m_ref[...]
        if tile_d is None:
            m_new = jnp.maximum(m_prev, jnp.max(s, axis=1, keepdims=True) + c)
            shift = m_new - c
        else:
            b = bias_ref[tile_d + 1]
            s = jnp.concatenate([s[:t] + b, s[t:] + b], axis=0)
            m_new = jnp.maximum(m_prev, jnp.max(s, axis=1, keepdims=True))
            shift = m_new
        alpha = jnp.exp2(m_prev - m_new)
        p = jnp.exp2(s - jnp.concatenate([shift] * (t // LANES), axis=1))
        l_ref[...] = alpha * l_ref[...] + jnp.sum(p, axis=1, keepdims=True)
        acc_ref[...] = alpha * acc_ref[...] + jnp.dot(p.astype(bf16), v, preferred_element_type=f32)
        m_ref[...] = m_new

    def far_sweep(lo, hi, c):
        pairs = (hi - lo) // 2

        def body(j, carry):
            chunk(lo + 2 * j, None, c)
            chunk(lo + 2 * j + 1, None, c)
            return carry
        lax.fori_loop(0, pairs, body, 0)

        @pl.when(lo + 2 * pairs < hi)
        def _():
            chunk(hi - 1, None, c)

    far_sweep(0, jnp.maximum(qi - 1, 0), cfar_ref[h, 0])
    for d in (-1, 0, 1):
        j = qi + d

        @pl.when(jnp.logical_and(j >= 0, j < nk))
        def _():
            chunk(j, d, None)
    far_sweep(jnp.minimum(qi + 2, nk), nk, cfar_ref[h, 1])

    lam = lam_ref[0]
    out = acc_ref[:t] / l_ref[:t] - lam * (acc_ref[t:] / l_ref[t:])
    o_ref[...] = (_rms(out, sw_ref[...], SUBLN_EPS) * lam_scale).astype(o_ref.dtype)


def _bias_tiles(rel_bias, t):
    span = 2 * t - 1
    rel_vec = np.arange(-span, span + 1)
    bias_vec = jnp.transpose(rel_bias[jnp.asarray(_bucket_of_rel(rel_vec))].astype(f32))
    tiles = []
    for d in (-1, 0, 1):
        w = lax.slice_in_dim(bias_vec, t + d * t, t + d * t + span, axis=1)
        wp = jnp.pad(w, ((0, 0), (0, 1)))
        skew = jnp.tile(wp, (1, t))[:, :t * span].reshape(-1, t, span)
        tiles.append(skew[:, :, t - 1:t - 1 + t])
    return jnp.stack(tiles, axis=1)


def _attention(proj3, lam, rel_bias, subln_w, lam_scale):
    B, S, _ = proj3.shape
    t = min(ATT_TILE, S)
    assert S % t == 0 and t >= MAX_DISTANCE
    bias_tiles = _bias_tiles(rel_bias, t) * LOG2E
    far = _bucket_of_rel(np.array([-MAX_DISTANCE, MAX_DISTANCE]))
    cfar = jnp.transpose(rel_bias[jnp.asarray(far)].astype(f32)) * LOG2E
    n_heads = N_ATT_HEADS
    nk = S // t
    return pl.pallas_call(
        functools.partial(_attn_body, lam_scale=lam_scale, t=t, nk=nk),
        out_shape=jax.ShapeDtypeStruct((B, S, ATT_WIDTH), bf16),
        grid=(B, n_heads, nk),
        in_specs=[pl.BlockSpec(memory_space=pltpu.SMEM),
                  pl.BlockSpec(memory_space=pltpu.SMEM),
                  pl.BlockSpec((None, t, LANES), lambda b, h, qi: (b, qi, h)),
                  pl.BlockSpec((None, S, LANES), lambda b, h, qi: (b, 0, n_heads + h)),
                  pl.BlockSpec((None, S, LANES), lambda b, h, qi: (b, 0, 2 * n_heads + h)),
                  pl.BlockSpec((None, 3, t, t), lambda b, h, qi: (h, 0, 0, 0)),
                  pl.BlockSpec((1, ATT_V_DIM), lambda b, h, qi: (0, 0))],
        out_specs=pl.BlockSpec((None, t, LANES), lambda b, h, qi: (b, qi, h)),
        scratch_shapes=[pltpu.VMEM((2 * t, LANES), bf16),
                        pltpu.VMEM((2 * t, LANES), f32), pltpu.VMEM((2 * t, LANES), f32),
                        pltpu.VMEM((2 * t, ATT_V_DIM), f32)],
        compiler_params=_params("parallel", "parallel", "arbitrary"),
        name="diff_attention",
    )(lam, cfar, proj3, proj3, proj3, bias_tiles, subln_w)


def _pool_body(x_ref, att_ref, p_ref, prev_ref, next_ref, pw_ref, ps_ref, wo_ref, o_ref, *, seq, tm):
    i = pl.program_id(0)
    t0 = (i * tm) % seq
    prev_ok = (t0 > 0).astype(f32)
    next_ok = (t0 + tm < seq).astype(f32)
    pm = p_ref[...].astype(f32)
    ext = jnp.concatenate([prev_ref[...].astype(f32) * prev_ok, pm,
                           next_ref[...].astype(f32) * next_ok], axis=0)
    pos = t0 + lax.broadcasted_iota(jnp.int32, (tm, 1), 0)
    pooled = []
    for g, w in enumerate(POOL_WINDOWS):
        cs = slice(g * POOL_GROUP_DIM, (g + 1) * POOL_GROUP_DIM)
        arr = ext[:, cs]
        n = 1
        while n < w:
            rows = arr.shape[0]
            arr = arr[0:rows - n] + arr[n:rows]
            n *= 2
        half = w // 2
        wsum = arr[POOL_HALO - half:POOL_HALO - half + tm]
        count = (jnp.minimum(pos + half, seq) - jnp.maximum(pos - half, 0)).astype(f32)
        mixed = (wsum / count - pm[:, cs]).astype(bf16)
        pooled.append((jnp.dot(mixed, pw_ref[g], preferred_element_type=f32) * ps_ref[:, cs]).astype(bf16))
    cat = jnp.concatenate([att_ref[...]] + pooled, axis=1)
    o_ref[...] = x_ref[...] + jnp.dot(cat, wo_ref[...], preferred_element_type=f32)


def _pool_outproj(x2d, att2d, proj, pool_w_bf, pool_scale, w_out_bf, seq):
    T = x2d.shape[0]
    tm = min(POOL_TM, seq)
    assert seq % tm == 0 and tm % POOL_HALO == 0
    hb = tm // POOL_HALO
    last = T // POOL_HALO - 1
    pcol = 3 * ATT_WIDTH // POOL_WIDTH
    return pl.pallas_call(
        functools.partial(_pool_body, seq=seq, tm=tm),
        out_shape=jax.ShapeDtypeStruct((T, D_MODEL), f32),
        grid=(T // tm,),
        in_specs=[pl.BlockSpec((tm, D_MODEL), lambda i: (i, 0)),
                  pl.BlockSpec((tm, ATT_WIDTH), lambda i: (i, 0)),
                  pl.BlockSpec((tm, POOL_WIDTH), lambda i: (i, pcol)),
                  pl.BlockSpec((POOL_HALO, POOL_WIDTH), lambda i: (jnp.maximum(i * hb - 1, 0), pcol)),
                  pl.BlockSpec((POOL_HALO, POOL_WIDTH), lambda i: (jnp.minimum((i + 1) * hb, last), pcol)),
                  pl.BlockSpec((len(POOL_WINDOWS), POOL_GROUP_DIM, POOL_GROUP_DIM), lambda i: (0, 0, 0)),
                  pl.BlockSpec((1, POOL_WIDTH), lambda i: (0, 0)),
                  pl.BlockSpec((D_MODEL, D_MODEL), lambda i: (0, 0))],
        out_specs=pl.BlockSpec((tm, D_MODEL), lambda i: (i, 0)),
        compiler_params=_params("parallel"),
        name="pool_outproj",
    )(x2d, att2d, proj, proj, proj, pool_w_bf, pool_scale, w_out_bf)


def _topk_rows(s, k, ids=None, id_bound=None):
    if ids is None:
        ids = lax.broadcasted_iota(jnp.int32, s.shape, 0)
        id_bound = s.shape[0]
    vals, picked = [], []
    for _ in range(k):
        m = jnp.max(s, axis=0, keepdims=True)
        pick = jnp.minimum(jnp.min(jnp.where(s == m, ids, id_bound), axis=0, keepdims=True), id_bound - 1)
        vals.append(m)
        picked.append(pick)
        s = jnp.where(ids == pick, -jnp.inf, s)
    return jnp.concatenate(vals, axis=0), jnp.concatenate(picked, axis=0)


def _candidate_grid(s1, s2):
    k, hk = PEER_TOPK, PEER_TOPK // 2
    m = s1.shape[1]
    iota_k = lax.broadcasted_iota(jnp.int32, (k, m), 0)
    iota_h = lax.broadcasted_iota(jnp.int32, (hk, m), 0)
    vals = [s1[0:1, :] + s2]
    ids = [iota_k]
    for a in range(1, hk):
        vals.append(s1[a:a + 1, :] + s2[0:hk, :])
        ids.append(iota_h + a * k)
    vals.append(s1[hk:k, :] + s2[0:1, :])
    ids.append((iota_h + hk) * k)
    return jnp.concatenate(vals, axis=0), jnp.concatenate(ids, axis=0)


def _select_rows(table, sel):
    out = jnp.zeros(sel.shape, table.dtype)
    for a in range(table.shape[0]):
        out = jnp.where(sel == a, table[a:a + 1, :], out)
    return out


def _route_body(x1_ref, nw_ref, wq_ref, keys_ref, h2_ref, e_ref, g_ref, qp_ref):
    h = pl.program_id(1)

    @pl.when(h == 0)
    def _():
        h2 = _rms(x1_ref[...], nw_ref[...], NORM_EPS)
        h2_ref[...] = h2
        qp = jnp.dot(h2.astype(bf16), wq_ref[...], preferred_element_type=f32).astype(bf16)
        for hh in range(PEER_HEADS):
            qp_ref[hh] = qp[:, hh * 2 * PEER_KEY_DIM:(hh + 1) * 2 * PEER_KEY_DIM]

    nt = (((1,), (1,)), ((), ()))
    tm = qp_ref.shape[1]
    for c0 in range(0, tm, LANES):
        cols = slice(c0, c0 + LANES)
        q = qp_ref[h, cols, :]
        halves = []
        for i in range(2):
            st = lax.dot_general(keys_ref[2 * h + i], q[:, i * PEER_KEY_DIM:(i + 1) * PEER_KEY_DIM], nt,
                                 preferred_element_type=f32)
            halves.append(_topk_rows(st, PEER_TOPK))
        (s1, i1), (s2, i2) = halves
        cand, flat_ids = _candidate_grid(s1, s2)
        best, flat = _topk_rows(cand, PEER_TOPK, flat_ids, PEER_TOPK * PEER_TOPK)
        e1 = _select_rows(i1, lax.shift_right_logical(flat, int(math.log2(PEER_TOPK))))
        e2 = _select_rows(i2, lax.bitwise_and(flat, PEER_TOPK - 1))
        e_ref[:, cols] = e1 * N_KEYS + e2
        ex = jnp.exp(best - jnp.max(best, axis=0, keepdims=True))
        g_ref[:, cols] = ex / jnp.sum(ex, axis=0, keepdims=True)


def _peer_route(x1, norm_w, wq_bf, keys_bf):
    T = x1.shape[0]
    tm = min(ROUTE_TM, T)
    return pl.pallas_call(
        _route_body,
        out_shape=(jax.ShapeDtypeStruct((T, D_MODEL), f32),
                   jax.ShapeDtypeStruct((PEER_SLOTS, T), jnp.int32),
                   jax.ShapeDtypeStruct((PEER_SLOTS, T), f32)),
        grid=(T // tm, PEER_HEADS),
        in_specs=[pl.BlockSpec((tm, D_MODEL), lambda i, h: (i, 0)),
                  pl.BlockSpec((1, D_MODEL), lambda i, h: (0, 0)),
                  pl.BlockSpec((D_MODEL, 2 * PEER_KEY_DIM * PEER_HEADS), lambda i, h: (0, 0)),
                  pl.BlockSpec((2 * PEER_HEADS, N_KEYS, PEER_KEY_DIM), lambda i, h: (0, 0, 0))],
        out_specs=(pl.BlockSpec((tm, D_MODEL), lambda i, h: (i, 0)),
                   pl.BlockSpec((PEER_TOPK, tm), lambda i, h: (h, i)),
                   pl.BlockSpec((PEER_TOPK, tm), lambda i, h: (h, i))),
        scratch_shapes=[pltpu.VMEM((PEER_HEADS, tm, 2 * PEER_KEY_DIM), bf16)],
        compiler_params=_params("parallel", "arbitrary"),
        name="peer_route",
    )(x1, norm_w, wq_bf, keys_bf)


def _pack_uv(u, v):
    ub = lax.bitcast_convert_type(u.astype(bf16), jnp.uint16).astype(u32)
    vb = lax.bitcast_convert_type(v.astype(bf16), jnp.uint16).astype(u32)
    return ((ub << 16) | vb).reshape(u.shape[0], D_CHUNKS, LANES)


def _peer_body(idx_ref, idx_next_ref, gates_ref, h2_ref, x1_ref, fw_ref, uv_hbm, o_ref, *scratch, tt):
    bufs, (sem, w_ref, row_ref) = scratch[:PEER_NBUF], scratch[PEER_NBUF:]
    i = pl.program_id(0)
    last_step = pl.num_programs(0) - 1

    def issue(iref, tn, b, lo=0, hi=PEER_SLOTS):
        for j in range(lo, hi):
            pltpu.make_async_copy(uv_hbm.at[iref[tn, j]], bufs[b].at[:, j, :], sem.at[b]).start()

    def wait_buf(b):
        pltpu.make_async_copy(uv_hbm.at[pl.ds(0, PEER_SLOTS)], bufs[b], sem.at[b]).wait()

    lane_t = lax.broadcasted_iota(jnp.int32, (PEER_SLOTS, tt), 1)
    hi_mask = jnp.asarray(0xFFFF0000, u32)

    per_chunk = PEER_SLOTS // D_CHUNKS
    chunk = lambda c: slice(c * LANES, (c + 1) * LANES)

    def act_chunk(b, hrow, c, acc):
        u = lax.bitcast_convert_type(bufs[b][c] & hi_mask, f32)
        return acc + u * hrow[:, chunk(c)]

    def finish_weights(t, acc):
        a = jnp.sum(acc, axis=1, keepdims=True)
        gate = jnp.sum(jnp.where(lane_t == t, gates_ref[...], 0.0), axis=1, keepdims=True)
        w = gate * (0.5 * a * (1.0 + lax.erf(a * (2.0 ** -0.5))))
        w_ref[...] = jnp.broadcast_to(w, (PEER_SLOTS, LANES))

    def step(t, b, iref, tn, has_next):
        b1 = (b + 1) % PEER_NBUF
        nb = (b + PEER_AHEAD) % PEER_NBUF
        if has_next:
            wait_buf(b1)
            hrow1 = h2_ref[pl.ds(t + 1, 1), :]
        wb = w_ref[...]
        xrow = x1_ref[pl.ds(t, 1), :]
        acc = jnp.zeros((PEER_SLOTS, LANES), f32)
        for c in range(D_CHUNKS):
            issue(iref, tn, nb, c * per_chunk, (c + 1) * per_chunk)
            if has_next:
                acc = act_chunk(b1, hrow1, c, acc)
            v = lax.bitcast_convert_type(bufs[b][c] << 16, f32)
            row_ref[:, chunk(c)] = xrow[:, chunk(c)] + jnp.sum(v * wb, axis=0, keepdims=True)
        if has_next:
            finish_weights(t + 1, acc)
        o_ref[pl.ds(t, 1), :] = _rms(row_ref[...], fw_ref[...], NORM_EPS)

    @pl.when(i == 0)
    def _():
        for t in range(PEER_AHEAD):
            issue(idx_ref, t, t)

    wait_buf(0)
    hrow0 = h2_ref[pl.ds(0, 1), :]
    acc0 = jnp.zeros((PEER_SLOTS, LANES), f32)
    for c in range(D_CHUNKS):
        acc0 = act_chunk(0, hrow0, c, acc0)
    finish_weights(0, acc0)

    groups = tt // PEER_NBUF

    def group(g, carry):
        for b in range(PEER_NBUF):
            t = g * PEER_NBUF + b
            step(t, b, idx_ref, t + PEER_AHEAD, True)
        return carry

    lax.fori_loop(0, groups - 1, group, 0)
    for b in range(PEER_NBUF):
        t = (groups - 1) * PEER_NBUF + b
        tn = t + PEER_AHEAD
        if tn < tt:
            step(t, b, idx_ref, tn, True)
        else:
            step(t, b, idx_next_ref, tn - tt, t + 1 < tt)

    @pl.when(i == last_step)
    def _():
        for t in range(PEER_AHEAD):
            wait_buf(t)


def _peer_apply(idx, gates_t, h2, x1, final_w, uv):
    T = x1.shape[0]
    tt = PEER_TT
    assert T % tt == 0 and tt % PEER_NBUF == 0 and tt >= 2 * PEER_NBUF
    nsteps = T // tt
    return pl.pallas_call(
        functools.partial(_peer_body, tt=tt),
        out_shape=jax.ShapeDtypeStruct((T, D_MODEL), f32),
        grid=(nsteps,),
        in_specs=[pl.BlockSpec((tt, PEER_SLOTS), lambda i: (i, 0), memory_space=pltpu.SMEM),
                  pl.BlockSpec((tt, PEER_SLOTS), lambda i: (jnp.minimum(i + 1, nsteps - 1), 0),
                               memory_space=pltpu.SMEM),
                  pl.BlockSpec((PEER_SLOTS, tt), lambda i: (0, i)),
                  pl.BlockSpec((tt, D_MODEL), lambda i: (i, 0)),
                  pl.BlockSpec((tt, D_MODEL), lambda i: (i, 0)),
                  pl.BlockSpec((1, D_MODEL), lambda i: (0, 0)),
                  pl.BlockSpec(memory_space=pl.ANY)],
        out_specs=pl.BlockSpec((tt, D_MODEL), lambda i: (i, 0)),
        scratch_shapes=[pltpu.VMEM((D_CHUNKS, PEER_SLOTS, LANES), u32) for _ in range(PEER_NBUF)]
        + [pltpu.SemaphoreType.DMA((PEER_NBUF,)),
           pltpu.VMEM((PEER_SLOTS, LANES), f32),
           pltpu.VMEM((1, D_MODEL), f32)],
        compiler_params=_params("arbitrary"),
        name="peer_apply",
    )(idx, idx, gates_t, h2, x1, final_w, uv)


def _trunk(x, rel_bias, norm1_w, w_in, lambda_q1, lambda_k1, lambda_q2, lambda_k2, subln_w,
           pool_w, pool_scale, w_out, norm2_w, peer_wq, peer_keys, peer_u, peer_v, final_norm_w):
    B, S, _ = x.shape
    T = B * S
    depth = norm1_w.shape[0]
    assert depth == 1, "the final norm is fused into the (single) layer's PEER stage"
    l = 0
    x2d = x.reshape(T, D_MODEL)
    lam_init = _lambda_init(l)
    lam = (jnp.exp(jnp.sum(lambda_q1[l].astype(f32) * lambda_k1[l].astype(f32)))
           - jnp.exp(jnp.sum(lambda_q2[l].astype(f32) * lambda_k2[l].astype(f32)))
           + lam_init).reshape(1)
    proj = _inproj(x2d, norm1_w[l].reshape(1, D_MODEL), w_in[l].astype(bf16))
    att = _attention(proj.reshape(B, S, IN_PROJ_DIM), lam, rel_bias, subln_w[l].reshape(1, ATT_V_DIM),
                     1.0 - lam_init)
    x1 = _pool_outproj(x2d, att.reshape(T, ATT_WIDTH), proj, pool_w[l].astype(bf16),
                       pool_scale[l].reshape(1, POOL_WIDTH), w_out[l].astype(bf16), S)
    keys = peer_keys[l].reshape(2 * PEER_HEADS, N_KEYS, PEER_KEY_DIM).astype(bf16)
    h2, expert_t, gates_t = _peer_route(x1, norm2_w[l].reshape(1, D_MODEL), peer_wq[l].astype(bf16), keys)
    y = _peer_apply(jnp.transpose(expert_t), gates_t, h2, x1, final_norm_w.reshape(1, D_MODEL),
                    _pack_uv(peer_u[l], peer_v[l]))
    return y.reshape(B, S, D_MODEL)


def kernel(x_prompt, x_sample, rel_bias, norm1_w, w_in, lambda_q1, lambda_k1, lambda_q2, lambda_k2, subln_w,
           pool_w, pool_scale, w_out, norm2_w, peer_wq, peer_keys, peer_u, peer_v, final_norm_w):
    args = (rel_bias, norm1_w, w_in, lambda_q1, lambda_k1, lambda_q2, lambda_k2, subln_w,
            pool_w, pool_scale, w_out, norm2_w, peer_wq, peer_keys, peer_u, peer_v, final_norm_w)
    return (_trunk(x_prompt, *args), _trunk(x_sample, *args))
```

```python
import functools
import math

import numpy as np
import jax
import jax.numpy as jnp
from jax import lax
from jax.experimental import pallas as pl
from jax.experimental.pallas import tpu as pltpu

f32 = jnp.float32
bf16 = jnp.bfloat16
u32 = jnp.uint32

D_MODEL = 2048
ATT_WIDTH = 1024
N_ATT_HEADS = 8
ATT_HEAD_DIM = 64
ATT_V_DIM = 128
POOL_WIDTH = 1024
POOL_WINDOWS = (2, 4, 8, 16)
POOL_GROUP_DIM = 256
IN_PROJ_DIM = 3 * ATT_WIDTH + POOL_WIDTH
N_BUCKETS = 32
MAX_DISTANCE = 128
N_KEYS = 128
PEER_HEADS = 8
PEER_TOPK = 16
PEER_KEY_DIM = 128
PEER_SLOTS = PEER_HEADS * PEER_TOPK
NORM_EPS = 1e-6
SUBLN_EPS = 1e-5
LOG2E = math.log2(math.e)

LANES = 128
BF16_SUBLANES = 16
VMEM_LIMIT = 56 * 1024 * 1024
D_CHUNKS = D_MODEL // LANES

ATT_TILE = 512
INPROJ_TM, INPROJ_TN = 512, 1024
POOL_TM = 256
ROUTE_TM = 256
PEER_TT = 128
PEER_NBUF = 8
PEER_AHEAD = PEER_NBUF - 1
POOL_HALO = BF16_SUBLANES


def _lambda_init(layer_idx):
    return 0.8 - 0.6 * math.exp(-0.3 * layer_idx)


def _rms(x, w, eps):
    ms = jnp.mean(x * x, axis=-1, keepdims=True)
    return x * lax.rsqrt(ms + eps) * w


def _params(*sem):
    return pltpu.CompilerParams(dimension_semantics=sem, vmem_limit_bytes=VMEM_LIMIT)


def _inproj_body(x_ref, nw_ref, w_ref, o_ref, h_ref):
    @pl.when(pl.program_id(1) == 0)
    def _():
        h_ref[...] = _rms(x_ref[...], nw_ref[...], NORM_EPS).astype(bf16)

    o_ref[...] = jnp.dot(h_ref[...], w_ref[...], preferred_element_type=f32).astype(o_ref.dtype)


def _inproj(x2d, norm_w, w_bf):
    T, N = x2d.shape[0], w_bf.shape[1]
    tm, tn = min(INPROJ_TM, T), INPROJ_TN
    return pl.pallas_call(
        _inproj_body,
        out_shape=jax.ShapeDtypeStruct((T, N), bf16),
        grid=(T // tm, N // tn),
        in_specs=[pl.BlockSpec((tm, D_MODEL), lambda i, j: (i, 0)),
                  pl.BlockSpec((1, D_MODEL), lambda i, j: (0, 0)),
                  pl.BlockSpec((D_MODEL, tn), lambda i, j: (0, j))],
        out_specs=pl.BlockSpec((tm, tn), lambda i, j: (i, j)),
        scratch_shapes=[pltpu.VMEM((tm, D_MODEL), bf16)],
        compiler_params=_params("parallel", "arbitrary"),
        name="inproj",
    )(x2d, norm_w, w_bf)


def _bucket_of_rel(rel):
    nb = N_BUCKETS // 2
    ret = (rel > 0).astype(np.int64) * nb
    n = np.abs(rel)
    max_exact = nb // 2
    large = max_exact + (np.log(np.maximum(n, 1).astype(np.float64) / max_exact)
                         / math.log(MAX_DISTANCE / max_exact) * (nb - max_exact)).astype(np.int64)
    large = np.minimum(large, nb - 1)
    return (ret + np.where(n < max_exact, n, large)).astype(np.int32)


def _attn_body(lam_ref, cfar_ref, q_ref, k_ref, v_ref, bias_ref, sw_ref, o_ref,
               qm_ref, m_ref, l_ref, acc_ref, *, lam_scale, t, nk):
    h, qi = pl.program_id(1), pl.program_id(2)
    m_ref[...] = jnp.full(m_ref.shape, -jnp.inf, f32)
    l_ref[...] = jnp.zeros(l_ref.shape, f32)
    acc_ref[...] = jnp.zeros(acc_ref.shape, f32)

    q = (q_ref[...].astype(f32) * (LOG2E * ATT_HEAD_DIM ** -0.5)).astype(bf16)
    lane = lax.broadcasted_iota(jnp.int32, q.shape, 1)
    qm_ref[:t] = jnp.where(lane < ATT_HEAD_DIM, q, jnp.zeros_like(q))
    qm_ref[t:] = jnp.where(lane >= ATT_HEAD_DIM, q, jnp.zeros_like(q))
    nt = (((1,), (1,)), ((), ()))

    def chunk(j, tile_d, c):
        start = pl.multiple_of(j * t, t)
        k = k_ref[pl.ds(start, t), :]
        v = v_ref[pl.ds(start, t), :]
        s = lax.dot_general(qm_ref[...], k, nt, preferred_element_type=f32)
        m_prev = m_ref[...]
        if tile_d is None:
            m_new = jnp.maximum(m_prev, jnp.max(s, axis=1, keepdims=True) + c)
            shift = m_new - c
        else:
            b = bias_ref[tile_d + 1]
            s = jnp.concatenate([s[:t] + b, s[t:] + b], axis=0)
            m_new = jnp.maximum(m_prev, jnp.max(s, axis=1, keepdims=True))
            shift = m_new
        alpha = jnp.exp2(m_prev - m_new)
        p = jnp.exp2(s - jnp.concatenate([shift] * (t // LANES), axis=1))
        l_ref[...] = alpha * l_ref[...] + jnp.sum(p, axis=1, keepdims=True)
        acc_ref[...] = alpha * acc_ref[...] + jnp.dot(p.astype(bf16), v, preferred_element_type=f32)
        m_ref[...] = m_new

    def far_sweep(lo, hi, c):
        pairs = (hi - lo) // 2

        def body(j, carry):
            chunk(lo + 2 * j, None, c)
            chunk(lo + 2 * j + 1, None, c)
            return carry
        lax.fori_loop(0, pairs, body, 0)

        @pl.when(lo + 2 * pairs < hi)
        def _():
            chunk(hi - 1, None, c)

    far_sweep(0, jnp.maximum(qi - 1, 0), cfar_ref[h, 0])
    for d in (-1, 0, 1):
        j = qi + d

        @pl.when(jnp.logical_and(j >= 0, j < nk))
        def _():
            chunk(j, d, None)
    far_sweep(jnp.minimum(qi + 2, nk), nk, cfar_ref[h, 1])

    lam = lam_ref[0]
    out = acc_ref[:t] / l_ref[:t] - lam * (acc_ref[t:] / l_ref[t:])
    o_ref[...] = (_rms(out, sw_ref[...], SUBLN_EPS) * lam_scale).astype(o_ref.dtype)


def _bias_tiles(rel_bias, t):
    span = 2 * t - 1
    rel_vec = np.arange(-span, span + 1)
    bias_vec = jnp.transpose(rel_bias[jnp.asarray(_bucket_of_rel(rel_vec))].astype(f32))
    tiles = []
    for d in (-1, 0, 1):
        w = lax.slice_in_dim(bias_vec, t + d * t, t + d * t + span, axis=1)
        wp = jnp.pad(w, ((0, 0), (0, 1)))
        skew = jnp.tile(wp, (1, t))[:, :t * span].reshape(-1, t, span)
        tiles.append(skew[:, :, t - 1:t - 1 + t])
    return jnp.stack(tiles, axis=1)


def _attention(proj3, lam, rel_bias, subln_w, lam_scale):
    B, S, _ = proj3.shape
    t = min(ATT_TILE, S)
    assert S % t == 0 and t >= MAX_DISTANCE
    bias_tiles = _bias_tiles(rel_bias, t) * LOG2E
    far = _bucket_of_rel(np.array([-MAX_DISTANCE, MAX_DISTANCE]))
    cfar = jnp.transpose(rel_bias[jnp.asarray(far)].astype(f32)) * LOG2E
    n_heads = N_ATT_HEADS
    nk = S // t
    return pl.pallas_call(
        functools.partial(_attn_body, lam_scale=lam_scale, t=t, nk=nk),
        out_shape=jax.ShapeDtypeStruct((B, S, ATT_WIDTH), bf16),
        grid=(B, n_heads, nk),
        in_specs=[pl.BlockSpec(memory_space=pltpu.SMEM),
                  pl.BlockSpec(memory_space=pltpu.SMEM),
                  pl.BlockSpec((None, t, LANES), lambda b, h, qi: (b, qi, h)),
                  pl.BlockSpec((None, S, LANES), lambda b, h, qi: (b, 0, n_heads + h)),
                  pl.BlockSpec((None, S, LANES), lambda b, h, qi: (b, 0, 2 * n_heads + h)),
                  pl.BlockSpec((None, 3, t, t), lambda b, h, qi: (h, 0, 0, 0)),
                  pl.BlockSpec((1, ATT_V_DIM), lambda b, h, qi: (0, 0))],
        out_specs=pl.BlockSpec((None, t, LANES), lambda b, h, qi: (b, qi, h)),
        scratch_shapes=[pltpu.VMEM((2 * t, LANES), bf16),
                        pltpu.VMEM((2 * t, LANES), f32), pltpu.VMEM((2 * t, LANES), f32),
                        pltpu.VMEM((2 * t, ATT_V_DIM), f32)],
        compiler_params=_params("parallel", "parallel", "arbitrary"),
        name="diff_attention",
    )(lam, cfar, proj3, proj3, proj3, bias_tiles, subln_w)


def _pool_body(x_ref, att_ref, p_ref, prev_ref, next_ref, pw_ref, ps_ref, wo_ref, o_ref, *, seq, tm):
    i = pl.program_id(0)
    t0 = (i * tm) % seq
    prev_ok = (t0 > 0).astype(f32)
    next_ok = (t0 + tm < seq).astype(f32)
    pm = p_ref[...].astype(f32)
    ext = jnp.concatenate([prev_ref[...].astype(f32) * prev_ok, pm,
                           next_ref[...].astype(f32) * next_ok], axis=0)
    pos = t0 + lax.broadcasted_iota(jnp.int32, (tm, 1), 0)
    pooled = []
    for g, w in enumerate(POOL_WINDOWS):
        cs = slice(g * POOL_GROUP_DIM, (g + 1) * POOL_GROUP_DIM)
        arr = ext[:, cs]
        n = 1
        while n < w:
            rows = arr.shape[0]
            arr = arr[0:rows - n] + arr[n:rows]
            n *= 2
        half = w // 2
        wsum = arr[POOL_HALO - half:POOL_HALO - half + tm]
        count = (jnp.minimum(pos + half, seq) - jnp.maximum(pos - half, 0)).astype(f32)
        mixed = (wsum / count - pm[:, cs]).astype(bf16)
        pooled.append((jnp.dot(mixed, pw_ref[g], preferred_element_type=f32) * ps_ref[:, cs]).astype(bf16))
    cat = jnp.concatenate([att_ref[...]] + pooled, axis=1)
    o_ref[...] = x_ref[...] + jnp.dot(cat, wo_ref[...], preferred_element_type=f32)


def _pool_outproj(x2d, att2d, proj, pool_w_bf, pool_scale, w_out_bf, seq):
    T = x2d.shape[0]
    tm = min(POOL_TM, seq)
    assert seq % tm == 0 and tm % POOL_HALO == 0
    hb = tm // POOL_HALO
    last = T // POOL_HALO - 1
    pcol = 3 * ATT_WIDTH // POOL_WIDTH
    return pl.pallas_call(
        functools.partial(_pool_body, seq=seq, tm=tm),
        out_shape=jax.ShapeDtypeStruct((T, D_MODEL), f32),
        grid=(T // tm,),
        in_specs=[pl.BlockSpec((tm, D_MODEL), lambda i: (i, 0)),
                  pl.BlockSpec((tm, ATT_WIDTH), lambda i: (i, 0)),
                  pl.BlockSpec((tm, POOL_WIDTH), lambda i: (i, pcol)),
                  pl.BlockSpec((POOL_HALO, POOL_WIDTH), lambda i: (jnp.maximum(i * hb - 1, 0), pcol)),
                  pl.BlockSpec((POOL_HALO, POOL_WIDTH), lambda i: (jnp.minimum((i + 1) * hb, last), pcol)),
                  pl.BlockSpec((len(POOL_WINDOWS), POOL_GROUP_DIM, POOL_GROUP_DIM), lambda i: (0, 0, 0)),
                  pl.BlockSpec((1, POOL_WIDTH), lambda i: (0, 0)),
                  pl.BlockSpec((D_MODEL, D_MODEL), lambda i: (0, 0))],
        out_specs=pl.BlockSpec((tm, D_MODEL), lambda i: (i, 0)),
        compiler_params=_params("parallel"),
        name="pool_outproj",
    )(x2d, att2d, proj, proj, proj, pool_w_bf, pool_scale, w_out_bf)


def _topk_rows(s, k, ids=None, id_bound=None):
    if ids is None:
        ids = lax.broadcasted_iota(jnp.int32, s.shape, 0)
        id_bound = s.shape[0]
    vals, picked = [], []
    for _ in range(k):
        m = jnp.max(s, axis=0, keepdims=True)
        pick = jnp.minimum(jnp.min(jnp.where(s == m, ids, id_bound), axis=0, keepdims=True), id_bound - 1)
        vals.append(m)
        picked.append(pick)
        s = jnp.where(ids == pick, -jnp.inf, s)
    return jnp.concatenate(vals, axis=0), jnp.concatenate(picked, axis=0)


def _candidate_grid(s1, s2):
    k, hk = PEER_TOPK, PEER_TOPK // 2
    m = s1.shape[1]
    iota_k = lax.broadcasted_iota(jnp.int32, (k, m), 0)
    iota_h = lax.broadcasted_iota(jnp.int32, (hk, m), 0)
    vals = [s1[0:1, :] + s2]
    ids = [iota_k]
    for a in range(1, hk):
        vals.append(s1[a:a + 1, :] + s2[0:hk, :])
        ids.append(iota_h + a * k)
    vals.append(s1[hk:k, :] + s2[0:1, :])
    ids.append((iota_h + hk) * k)
    return jnp.concatenate(vals, axis=0), jnp.concatenate(ids, axis=0)


def _select_rows(table, sel):
    out = jnp.zeros(sel.shape, table.dtype)
    for a in range(table.shape[0]):
        out = jnp.where(sel == a, table[a:a + 1, :], out)
    return out


def _route_body(x1_ref, nw_ref, wq_ref, keys_ref, h2_ref, e_ref, g_ref, qp_ref):
    h = pl.program_id(1)

    @pl.when(h == 0)
    def _():
        h2 = _rms(x1_ref[...], nw_ref[...], NORM_EPS)
        h2_ref[...] = h2
        qp = jnp.dot(h2.astype(bf16), wq_ref[...], preferred_element_type=f32).astype(bf16)
        for hh in range(PEER_HEADS):
            qp_ref[hh] = qp[:, hh * 2 * PEER_KEY_DIM:(hh + 1) * 2 * PEER_KEY_DIM]

    nt = (((1,), (1,)), ((), ()))
    tm = qp_ref.shape[1]
    for c0 in range(0, tm, LANES):
        cols = slice(c0, c0 + LANES)
        q = qp_ref[h, cols, :]
        halves = []
        for i in range(2):
            st = lax.dot_general(keys_ref[2 * h + i], q[:, i * PEER_KEY_DIM:(i + 1) * PEER_KEY_DIM], nt,
                                 preferred_element_type=f32)
            halves.append(_topk_rows(st, PEER_TOPK))
        (s1, i1), (s2, i2) = halves
        cand, flat_ids = _candidate_grid(s1, s2)
        best, flat = _topk_rows(cand, PEER_TOPK, flat_ids, PEER_TOPK * PEER_TOPK)
        e1 = _select_rows(i1, lax.shift_right_logical(flat, int(math.log2(PEER_TOPK))))
        e2 = _select_rows(i2, lax.bitwise_and(flat, PEER_TOPK - 1))
        e_ref[:, cols] = e1 * N_KEYS + e2
        ex = jnp.exp(best - jnp.max(best, axis=0, keepdims=True))
        g_ref[:, cols] = ex / jnp.sum(ex, axis=0, keepdims=True)


def _peer_route(x1, norm_w, wq_bf, keys_bf):
    T = x1.shape[0]
    tm = min(ROUTE_TM, T)
    return pl.pallas_call(
        _route_body,
        out_shape=(jax.ShapeDtypeStruct((T, D_MODEL), f32),
                   jax.ShapeDtypeStruct((PEER_SLOTS, T), jnp.int32),
                   jax.ShapeDtypeStruct((PEER_SLOTS, T), f32)),
        grid=(T // tm, PEER_HEADS),
        in_specs=[pl.BlockSpec((tm, D_MODEL), lambda i, h: (i, 0)),
                  pl.BlockSpec((1, D_MODEL), lambda i, h: (0, 0)),
                  pl.BlockSpec((D_MODEL, 2 * PEER_KEY_DIM * PEER_HEADS), lambda i, h: (0, 0)),
                  pl.BlockSpec((2 * PEER_HEADS, N_KEYS, PEER_KEY_DIM), lambda i, h: (0, 0, 0))],
        out_specs=(pl.BlockSpec((tm, D_MODEL), lambda i, h: (i, 0)),
                   pl.BlockSpec((PEER_TOPK, tm), lambda i, h: (h, i)),
                   pl.BlockSpec((PEER_TOPK, tm), lambda i, h: (h, i))),
        scratch_shapes=[pltpu.VMEM((PEER_HEADS, tm, 2 * PEER_KEY_DIM), bf16)],
        compiler_params=_params("parallel", "arbitrary"),
        name="peer_route",
    )(x1, norm_w, wq_bf, keys_bf)


def _pack_uv(u, v):
    ub = lax.bitcast_convert_type(u.astype(bf16), jnp.uint16).astype(u32)
    vb = lax.bitcast_convert_type(v.astype(bf16), jnp.uint16).astype(u32)
    return ((ub << 16) | vb).reshape(u.shape[0], D_CHUNKS, LANES)


def _peer_body(idx_ref, idx_next_ref, gates_ref, h2_ref, x1_ref, fw_ref, uv_hbm, o_ref, *scratch, tt):
    bufs, (sem, w_ref, row_ref) = scratch[:PEER_NBUF], scratch[PEER_NBUF:]
    i = pl.program_id(0)
    last_step = pl.num_programs(0) - 1

    def issue(iref, tn, b, lo=0, hi=PEER_SLOTS):
        for j in range(lo, hi):
            pltpu.async_copy(uv_hbm.at[iref[tn, j]], bufs[b].at[:, j, :], sem.at[b], priority=j % 2)

    def wait_buf(b):
        pltpu.make_async_copy(uv_hbm.at[pl.ds(0, PEER_SLOTS)], bufs[b], sem.at[b]).wait()

    lane_t = lax.broadcasted_iota(jnp.int32, (PEER_SLOTS, tt), 1)
    hi_mask = jnp.asarray(0xFFFF0000, u32)

    per_chunk = PEER_SLOTS // D_CHUNKS
    chunk = lambda c: slice(c * LANES, (c + 1) * LANES)

    def act_chunk(b, hrow, c, acc):
        u = lax.bitcast_convert_type(bufs[b][c] & hi_mask, f32)
        return acc + u * hrow[:, chunk(c)]

    def finish_weights(t, acc):
        a = jnp.sum(acc, axis=1, keepdims=True)
        gate = jnp.sum(jnp.where(lane_t == t, gates_ref[...], 0.0), axis=1, keepdims=True)
        w = gate * (0.5 * a * (1.0 + lax.erf(a * (2.0 ** -0.5))))
        w_ref[...] = jnp.broadcast_to(w, (PEER_SLOTS, LANES))

    def step(t, b, iref, tn, has_next):
        b1 = (b + 1) % PEER_NBUF
        nb = (b + PEER_AHEAD) % PEER_NBUF
        if has_next:
            wait_buf(b1)
            hrow1 = h2_ref[pl.ds(t + 1, 1), :]
        wb = w_ref[...]
        xrow = x1_ref[pl.ds(t, 1), :]
        acc = jnp.zeros((PEER_SLOTS, LANES), f32)
        for c in range(D_CHUNKS):
            issue(iref, tn, nb, c * per_chunk, (c + 1) * per_chunk)
            if has_next:
                acc = act_chunk(b1, hrow1, c, acc)
            v = lax.bitcast_convert_type(bufs[b][c] << 16, f32)
            row_ref[:, chunk(c)] = xrow[:, chunk(c)] + jnp.sum(v * wb, axis=0, keepdims=True)
        if has_next:
            finish_weights(t + 1, acc)
        o_ref[pl.ds(t, 1), :] = _rms(row_ref[...], fw_ref[...], NORM_EPS)

    @pl.when(i == 0)
    def _():
        for t in range(PEER_AHEAD):
            issue(idx_ref, t, t)

    wait_buf(0)
    hrow0 = h2_ref[pl.ds(0, 1), :]
    acc0 = jnp.zeros((PEER_SLOTS, LANES), f32)
    for c in range(D_CHUNKS):
        acc0 = act_chunk(0, hrow0, c, acc0)
    finish_weights(0, acc0)

    groups = tt // PEER_NBUF

    def group(g, carry):
        for b in range(PEER_NBUF):
            t = g * PEER_NBUF + b
            step(t, b, idx_ref, t + PEER_AHEAD, True)
        return carry

    lax.fori_loop(0, groups - 1, group, 0)
    for b in range(PEER_NBUF):
        t = (groups - 1) * PEER_NBUF + b
        tn = t + PEER_AHEAD
        if tn < tt:
            step(t, b, idx_ref, tn, True)
        else:
            step(t, b, idx_next_ref, tn - tt, t + 1 < tt)

    @pl.when(i == last_step)
    def _():
        for t in range(PEER_AHEAD):
            wait_buf(t)


def _peer_apply(idx, gates_t, h2, x1, final_w, uv):
    T = x1.shape[0]
    tt = PEER_TT
    assert T % tt == 0 and tt % PEER_NBUF == 0 and tt >= 2 * PEER_NBUF
    nsteps = T // tt
    return pl.pallas_call(
        functools.partial(_peer_body, tt=tt),
        out_shape=jax.ShapeDtypeStruct((T, D_MODEL), f32),
        grid=(nsteps,),
        in_specs=[pl.BlockSpec((tt, PEER_SLOTS), lambda i: (i, 0), memory_space=pltpu.SMEM),
                  pl.BlockSpec((tt, PEER_SLOTS), lambda i: (jnp.minimum(i + 1, nsteps - 1), 0),
                               memory_space=pltpu.SMEM),
                  pl.BlockSpec((PEER_SLOTS, tt), lambda i: (0, i)),
                  pl.BlockSpec((tt, D_MODEL), lambda i: (i, 0)),
                  pl.BlockSpec((tt, D_MODEL), lambda i: (i, 0)),
                  pl.BlockSpec((1, D_MODEL), lambda i: (0, 0)),
                  pl.BlockSpec(memory_space=pl.ANY)],
        out_specs=pl.BlockSpec((tt, D_MODEL), lambda i: (i, 0)),
        scratch_shapes=[pltpu.VMEM((D_CHUNKS, PEER_SLOTS, LANES), u32) for _ in range(PEER_NBUF)]
        + [pltpu.SemaphoreType.DMA((PEER_NBUF,)),
           pltpu.VMEM((PEER_SLOTS, LANES), f32),
           pltpu.VMEM((1, D_MODEL), f32)],
        compiler_params=_params("arbitrary"),
        name="peer_apply",
    )(idx, idx, gates_t, h2, x1, final_w, uv)


def _trunk(x, rel_bias, norm1_w, w_in, lambda_q1, lambda_k1, lambda_q2, lambda_k2, subln_w,
           pool_w, pool_scale, w_out, norm2_w, peer_wq, peer_keys, peer_u, peer_v, final_norm_w):
    B, S, _ = x.shape
    T = B * S
    depth = norm1_w.shape[0]
    assert depth == 1, "the final norm is fused into the (single) layer's PEER stage"
    l = 0
    x2d = x.reshape(T, D_MODEL)
    lam_init = _lambda_init(l)
    lam = (jnp.exp(jnp.sum(lambda_q1[l].astype(f32) * lambda_k1[l].astype(f32)))
           - jnp.exp(jnp.sum(lambda_q2[l].astype(f32) * lambda_k2[l].astype(f32)))
           + lam_init).reshape(1)
    proj = _inproj(x2d, norm1_w[l].reshape(1, D_MODEL), w_in[l].astype(bf16))
    att = _attention(proj.reshape(B, S, IN_PROJ_DIM), lam, rel_bias, subln_w[l].reshape(1, ATT_V_DIM),
                     1.0 - lam_init)
    x1 = _pool_outproj(x2d, att.reshape(T, ATT_WIDTH), proj, pool_w[l].astype(bf16),
                       pool_scale[l].reshape(1, POOL_WIDTH), w_out[l].astype(bf16), S)
    keys = peer_keys[l].reshape(2 * PEER_HEADS, N_KEYS, PEER_KEY_DIM).astype(bf16)
    h2, expert_t, gates_t = _peer_route(x1, norm2_w[l].reshape(1, D_MODEL), peer_wq[l].astype(bf16), keys)
    y = _peer_apply(jnp.transpose(expert_t), gates_t, h2, x1, final_norm_w.reshape(1, D_MODEL),
                    _pack_uv(peer_u[l], peer_v[l]))
    return y.reshape(B, S, D_MODEL)


def kernel(x_prompt, x_sample, rel_bias, norm1_w, w_in, lambda_q1, lambda_k1, lambda_q2, lambda_k2, subln_w,
           pool_w, pool_scale, w_out, norm2_w, peer_wq, peer_keys, peer_u, peer_v, final_norm_w):
    args = (rel_bias, norm1_w, w_in, lambda_q1, lambda_k1, lambda_q2, lambda_k2, subln_w,
            pool_w, pool_scale, w_out, norm2_w, peer_wq, peer_keys, peer_u, peer_v, final_norm_w)
    return (_trunk(x_prompt, *args), _trunk(x_sample, *args))
```

```python
import functools
import math

import numpy as np
import jax
import jax.numpy as jnp
from jax import lax
from jax.experimental import pallas as pl
from jax.experimental.pallas import tpu as pltpu

f32 = jnp.float32
bf16 = jnp.bfloat16
u32 = jnp.uint32

D_MODEL = 2048
ATT_WIDTH = 1024
N_ATT_HEADS = 8
ATT_HEAD_DIM = 64
ATT_V_DIM = 128
POOL_WIDTH = 1024
POOL_WINDOWS = (2, 4, 8, 16)
POOL_GROUP_DIM = 256
IN_PROJ_DIM = 3 * ATT_WIDTH + POOL_WIDTH
N_BUCKETS = 32
MAX_DISTANCE = 128
N_KEYS = 128
PEER_HEADS = 8
PEER_TOPK = 16
PEER_KEY_DIM = 128
PEER_SLOTS = PEER_HEADS * PEER_TOPK
NORM_EPS = 1e-6
SUBLN_EPS = 1e-5
LOG2E = math.log2(math.e)

LANES = 128
BF16_SUBLANES = 16
VMEM_LIMIT = 56 * 1024 * 1024
D_CHUNKS = D_MODEL // LANES

ATT_TILE = 512
INPROJ_TM, INPROJ_TN = 512, 1024
POOL_TM = 256
ROUTE_TM = 256
ROUTE_HEADS = 4
PEER_TT = 128
PEER_NBUF = 8
PEER_AHEAD = PEER_NBUF - 1
POOL_HALO = BF16_SUBLANES


def _lambda_init(layer_idx):
    return 0.8 - 0.6 * math.exp(-0.3 * layer_idx)


def _rms(x, w, eps):
    ms = jnp.mean(x * x, axis=-1, keepdims=True)
    return x * lax.rsqrt(ms + eps) * w


def _params(*sem):
    return pltpu.CompilerParams(dimension_semantics=sem, vmem_limit_bytes=VMEM_LIMIT)


def _inproj_body(x_ref, nw_ref, w_ref, o_ref, h_ref):
    @pl.when(pl.program_id(1) == 0)
    def _():
        h_ref[...] = _rms(x_ref[...], nw_ref[...], NORM_EPS).astype(bf16)

    o_ref[...] = jnp.dot(h_ref[...], w_ref[...], preferred_element_type=f32).astype(o_ref.dtype)


def _inproj(x2d, norm_w, w_bf):
    T, N = x2d.shape[0], w_bf.shape[1]
    tm, tn = min(INPROJ_TM, T), INPROJ_TN
    return pl.pallas_call(
        _inproj_body,
        out_shape=jax.ShapeDtypeStruct((T, N), bf16),
        grid=(T // tm, N // tn),
        in_specs=[pl.BlockSpec((tm, D_MODEL), lambda i, j: (i, 0)),
                  pl.BlockSpec((1, D_MODEL), lambda i, j: (0, 0)),
                  pl.BlockSpec((D_MODEL, tn), lambda i, j: (0, j))],
        out_specs=pl.BlockSpec((tm, tn), lambda i, j: (i, j)),
        scratch_shapes=[pltpu.VMEM((tm, D_MODEL), bf16)],
        compiler_params=_params("parallel", "arbitrary"),
        name="inproj",
    )(x2d, norm_w, w_bf)


def _bucket_of_rel(rel):
    nb = N_BUCKETS // 2
    ret = (rel > 0).astype(np.int64) * nb
    n = np.abs(rel)
    max_exact = nb // 2
    large = max_exact + (np.log(np.maximum(n, 1).astype(np.float64) / max_exact)
                         / math.log(MAX_DISTANCE / max_exact) * (nb - max_exact)).astype(np.int64)
    large = np.minimum(large, nb - 1)
    return (ret + np.where(n < max_exact, n, large)).astype(np.int32)


def _attn_body(lam_ref, cfar_ref, q_ref, k_ref, v_ref, bias_ref, sw_ref, o_ref,
               qm_ref, m_ref, l_ref, acc_ref, *, lam_scale, t, nk):
    h, qi = pl.program_id(1), pl.program_id(2)
    m_ref[...] = jnp.full(m_ref.shape, -jnp.inf, f32)
    l_ref[...] = jnp.zeros(l_ref.shape, f32)
    acc_ref[...] = jnp.zeros(acc_ref.shape, f32)

    q = (q_ref[...].astype(f32) * (LOG2E * ATT_HEAD_DIM ** -0.5)).astype(bf16)
    lane = lax.broadcasted_iota(jnp.int32, q.shape, 1)
    qm_ref[:t] = jnp.where(lane < ATT_HEAD_DIM, q, jnp.zeros_like(q))
    qm_ref[t:] = jnp.where(lane >= ATT_HEAD_DIM, q, jnp.zeros_like(q))
    nt = (((1,), (1,)), ((), ()))

    def chunk(j, tile_d, c):
        start = pl.multiple_of(j * t, t)
        k = k_ref[pl.ds(start, t), :]
        v = v_ref[pl.ds(start, t), :]
        s = lax.dot_general(qm_ref[...], k, nt, preferred_element_type=f32)
        m_prev = m_ref[...]
        if tile_d is None:
            m_new = jnp.maximum(m_prev, jnp.max(s, axis=1, keepdims=True) + c)
            shift = m_new - c
        else:
            b = bias_ref[tile_d + 1]
            s = jnp.concatenate([s[:t] + b, s[t:] + b], axis=0)
            m_new = jnp.maximum(m_prev, jnp.max(s, axis=1, keepdims=True))
            shift = m_new
        alpha = jnp.exp2(m_prev - m_new)
        p = jnp.exp2(s - jnp.concatenate([shift] * (t // LANES), axis=1))
        l_ref[...] = alpha * l_ref[...] + jnp.sum(p, axis=1, keepdims=True)
        acc_ref[...] = alpha * acc_ref[...] + jnp.dot(p.astype(bf16), v, preferred_element_type=f32)
        m_ref[...] = m_new

    def far_sweep(lo, hi, c):
        pairs = (hi - lo) // 2

        def body(j, carry):
            chunk(lo + 2 * j, None, c)
            chunk(lo + 2 * j + 1, None, c)
            return carry
        lax.fori_loop(0, pairs, body, 0)

        @pl.when(lo + 2 * pairs < hi)
        def _():
            chunk(hi - 1, None, c)

    far_sweep(0, jnp.maximum(qi - 1, 0), cfar_ref[h, 0])
    for d in (-1, 0, 1):
        j = qi + d

        @pl.when(jnp.logical_and(j >= 0, j < nk))
        def _():
            chunk(j, d, None)
    far_sweep(jnp.minimum(qi + 2, nk), nk, cfar_ref[h, 1])

    lam = lam_ref[0]
    out = acc_ref[:t] / l_ref[:t] - lam * (acc_ref[t:] / l_ref[t:])
    o_ref[...] = (_rms(out, sw_ref[...], SUBLN_EPS) * lam_scale).astype(o_ref.dtype)


def _bias_tiles(rel_bias, t):
    span = 2 * t - 1
    rel_vec = np.arange(-span, span + 1)
    bias_vec = jnp.transpose(rel_bias[jnp.asarray(_bucket_of_rel(rel_vec))].astype(f32))
    tiles = []
    for d in (-1, 0, 1):
        w = lax.slice_in_dim(bias_vec, t + d * t, t + d * t + span, axis=1)
        wp = jnp.pad(w, ((0, 0), (0, 1)))
        skew = jnp.tile(wp, (1, t))[:, :t * span].reshape(-1, t, span)
        tiles.append(skew[:, :, t - 1:t - 1 + t])
    return jnp.stack(tiles, axis=1)


def _attention(proj3, lam, rel_bias, subln_w, lam_scale):
    B, S, _ = proj3.shape
    t = min(ATT_TILE, S)
    assert S % t == 0 and t >= MAX_DISTANCE
    bias_tiles = _bias_tiles(rel_bias, t) * LOG2E
    far = _bucket_of_rel(np.array([-MAX_DISTANCE, MAX_DISTANCE]))
    cfar = jnp.transpose(rel_bias[jnp.asarray(far)].astype(f32)) * LOG2E
    n_heads = N_ATT_HEADS
    nk = S // t
    return pl.pallas_call(
        functools.partial(_attn_body, lam_scale=lam_scale, t=t, nk=nk),
        out_shape=jax.ShapeDtypeStruct((B, S, ATT_WIDTH), bf16),
        grid=(B, n_heads, nk),
        in_specs=[pl.BlockSpec(memory_space=pltpu.SMEM),
                  pl.BlockSpec(memory_space=pltpu.SMEM),
                  pl.BlockSpec((None, t, LANES), lambda b, h, qi: (b, qi, h)),
                  pl.BlockSpec((None, S, LANES), lambda b, h, qi: (b, 0, n_heads + h)),
                  pl.BlockSpec((None, S, LANES), lambda b, h, qi: (b, 0, 2 * n_heads + h)),
                  pl.BlockSpec((None, 3, t, t), lambda b, h, qi: (h, 0, 0, 0)),
                  pl.BlockSpec((1, ATT_V_DIM), lambda b, h, qi: (0, 0))],
        out_specs=pl.BlockSpec((None, t, LANES), lambda b, h, qi: (b, qi, h)),
        scratch_shapes=[pltpu.VMEM((2 * t, LANES), bf16),
                        pltpu.VMEM((2 * t, LANES), f32), pltpu.VMEM((2 * t, LANES), f32),
                        pltpu.VMEM((2 * t, ATT_V_DIM), f32)],
        compiler_params=_params("parallel", "parallel", "arbitrary"),
        name="diff_attention",
    )(lam, cfar, proj3, proj3, proj3, bias_tiles, subln_w)


def _pool_body(x_ref, att_ref, p_ref, prev_ref, next_ref, pw_ref, ps_ref, wo_ref, o_ref, *, seq, tm):
    i = pl.program_id(0)
    t0 = (i * tm) % seq
    prev_ok = (t0 > 0).astype(f32)
    next_ok = (t0 + tm < seq).astype(f32)
    pm = p_ref[...].astype(f32)
    ext = jnp.concatenate([prev_ref[...].astype(f32) * prev_ok, pm,
                           next_ref[...].astype(f32) * next_ok], axis=0)
    pos = t0 + lax.broadcasted_iota(jnp.int32, (tm, 1), 0)
    pooled = []
    for g, w in enumerate(POOL_WINDOWS):
        cs = slice(g * POOL_GROUP_DIM, (g + 1) * POOL_GROUP_DIM)
        arr = ext[:, cs]
        n = 1
        while n < w:
            rows = arr.shape[0]
            arr = arr[0:rows - n] + arr[n:rows]
            n *= 2
        half = w // 2
        wsum = arr[POOL_HALO - half:POOL_HALO - half + tm]
        count = (jnp.minimum(pos + half, seq) - jnp.maximum(pos - half, 0)).astype(f32)
        mixed = (wsum / count - pm[:, cs]).astype(bf16)
        pooled.append((jnp.dot(mixed, pw_ref[g], preferred_element_type=f32) * ps_ref[:, cs]).astype(bf16))
    cat = jnp.concatenate([att_ref[...]] + pooled, axis=1)
    o_ref[...] = x_ref[...] + jnp.dot(cat, wo_ref[...], preferred_element_type=f32)


def _pool_outproj(x2d, att2d, proj, pool_w_bf, pool_scale, w_out_bf, seq):
    T = x2d.shape[0]
    tm = min(POOL_TM, seq)
    assert seq % tm == 0 and tm % POOL_HALO == 0
    hb = tm // POOL_HALO
    last = T // POOL_HALO - 1
    pcol = 3 * ATT_WIDTH // POOL_WIDTH
    return pl.pallas_call(
        functools.partial(_pool_body, seq=seq, tm=tm),
        out_shape=jax.ShapeDtypeStruct((T, D_MODEL), f32),
        grid=(T // tm,),
        in_specs=[pl.BlockSpec((tm, D_MODEL), lambda i: (i, 0)),
                  pl.BlockSpec((tm, ATT_WIDTH), lambda i: (i, 0)),
                  pl.BlockSpec((tm, POOL_WIDTH), lambda i: (i, pcol)),
                  pl.BlockSpec((POOL_HALO, POOL_WIDTH), lambda i: (jnp.maximum(i * hb - 1, 0), pcol)),
                  pl.BlockSpec((POOL_HALO, POOL_WIDTH), lambda i: (jnp.minimum((i + 1) * hb, last), pcol)),
                  pl.BlockSpec((len(POOL_WINDOWS), POOL_GROUP_DIM, POOL_GROUP_DIM), lambda i: (0, 0, 0)),
                  pl.BlockSpec((1, POOL_WIDTH), lambda i: (0, 0)),
                  pl.BlockSpec((D_MODEL, D_MODEL), lambda i: (0, 0))],
        out_specs=pl.BlockSpec((tm, D_MODEL), lambda i: (i, 0)),
        compiler_params=_params("parallel"),
        name="pool_outproj",
    )(x2d, att2d, proj, proj, proj, pool_w_bf, pool_scale, w_out_bf)


def _topk_rows(s, k, ids=None, id_bound=None):
    return _topk_rows_lockstep([s], k, ids, id_bound)[0]


def _topk_rows_lockstep(arrays, k, ids=None, id_bound=None):
    if ids is None:
        ids = lax.broadcasted_iota(jnp.int32, arrays[0].shape, 0)
        id_bound = arrays[0].shape[0]
    arrays = list(arrays)
    vals = [[] for _ in arrays]
    picked = [[] for _ in arrays]
    for _ in range(k):
        for n, s in enumerate(arrays):
            m = jnp.max(s, axis=0, keepdims=True)
            pick = jnp.minimum(jnp.min(jnp.where(s == m, ids, id_bound), axis=0, keepdims=True), id_bound - 1)
            vals[n].append(m)
            picked[n].append(pick)
            arrays[n] = jnp.where(ids == pick, -jnp.inf, s)
    return [(jnp.concatenate(v, axis=0), jnp.concatenate(p, axis=0)) for v, p in zip(vals, picked)]


def _candidate_grid(s1, s2):
    k, hk = PEER_TOPK, PEER_TOPK // 2
    m = s1.shape[1]
    iota_k = lax.broadcasted_iota(jnp.int32, (k, m), 0)
    iota_h = lax.broadcasted_iota(jnp.int32, (hk, m), 0)
    vals = [s1[0:1, :] + s2]
    ids = [iota_k]
    for a in range(1, hk):
        vals.append(s1[a:a + 1, :] + s2[0:hk, :])
        ids.append(iota_h + a * k)
    vals.append(s1[hk:k, :] + s2[0:1, :])
    ids.append((iota_h + hk) * k)
    return jnp.concatenate(vals, axis=0), jnp.concatenate(ids, axis=0)


def _select_rows(table, sel):
    out = jnp.zeros(sel.shape, table.dtype)
    for a in range(table.shape[0]):
        out = jnp.where(sel == a, table[a:a + 1, :], out)
    return out


def _route_body(x1_ref, nw_ref, wq_ref, keys_ref, h2_ref, e_ref, g_ref, qp_ref):
    h = pl.program_id(1)

    @pl.when(h == 0)
    def _():
        h2 = _rms(x1_ref[...], nw_ref[...], NORM_EPS)
        h2_ref[...] = h2
        qp = jnp.dot(h2.astype(bf16), wq_ref[...], preferred_element_type=f32).astype(bf16)
        for hh in range(PEER_HEADS):
            qp_ref[hh] = qp[:, hh * 2 * PEER_KEY_DIM:(hh + 1) * 2 * PEER_KEY_DIM]

    nt = (((1,), (1,)), ((), ()))
    tm = qp_ref.shape[1]
    for c0 in range(0, tm, LANES):
        cols = slice(c0, c0 + LANES)
        scores = []
        for g in range(ROUTE_HEADS):
            q = qp_ref[h * ROUTE_HEADS + g, cols, :]
            for i in range(2):
                scores.append(lax.dot_general(
                    keys_ref[2 * (h * ROUTE_HEADS + g) + i], q[:, i * PEER_KEY_DIM:(i + 1) * PEER_KEY_DIM], nt,
                    preferred_element_type=f32))
        halves = _topk_rows_lockstep(scores, PEER_TOPK)
        grids = [_candidate_grid(halves[2 * g][0], halves[2 * g + 1][0]) for g in range(ROUTE_HEADS)]
        flat_ids = grids[0][1]
        tops = _topk_rows_lockstep([c for c, _ in grids], PEER_TOPK, flat_ids, PEER_TOPK * PEER_TOPK)
        for g, (best, flat) in enumerate(tops):
            i1, i2 = halves[2 * g][1], halves[2 * g + 1][1]
            e1 = _select_rows(i1, lax.shift_right_logical(flat, int(math.log2(PEER_TOPK))))
            e2 = _select_rows(i2, lax.bitwise_and(flat, PEER_TOPK - 1))
            rows = slice(g * PEER_TOPK, (g + 1) * PEER_TOPK)
            e_ref[rows, cols] = e1 * N_KEYS + e2
            ex = jnp.exp(best - jnp.max(best, axis=0, keepdims=True))
            g_ref[rows, cols] = ex / jnp.sum(ex, axis=0, keepdims=True)


def _peer_route(x1, norm_w, wq_bf, keys_bf):
    T = x1.shape[0]
    tm = min(ROUTE_TM, T)
    return pl.pallas_call(
        _route_body,
        out_shape=(jax.ShapeDtypeStruct((T, D_MODEL), f32),
                   jax.ShapeDtypeStruct((PEER_SLOTS, T), jnp.int32),
                   jax.ShapeDtypeStruct((PEER_SLOTS, T), f32)),
        grid=(T // tm, PEER_HEADS // ROUTE_HEADS),
        in_specs=[pl.BlockSpec((tm, D_MODEL), lambda i, h: (i, 0)),
                  pl.BlockSpec((1, D_MODEL), lambda i, h: (0, 0)),
                  pl.BlockSpec((D_MODEL, 2 * PEER_KEY_DIM * PEER_HEADS), lambda i, h: (0, 0)),
                  pl.BlockSpec((2 * PEER_HEADS, N_KEYS, PEER_KEY_DIM), lambda i, h: (0, 0, 0))],
        out_specs=(pl.BlockSpec((tm, D_MODEL), lambda i, h: (i, 0)),
                   pl.BlockSpec((ROUTE_HEADS * PEER_TOPK, tm), lambda i, h: (h, i)),
                   pl.BlockSpec((ROUTE_HEADS * PEER_TOPK, tm), lambda i, h: (h, i))),
        scratch_shapes=[pltpu.VMEM((PEER_HEADS, tm, 2 * PEER_KEY_DIM), bf16)],
        compiler_params=_params("parallel", "arbitrary"),
        name="peer_route",
    )(x1, norm_w, wq_bf, keys_bf)


def _pack_uv(u, v):
    ub = lax.bitcast_convert_type(u.astype(bf16), jnp.uint16).astype(u32)
    vb = lax.bitcast_convert_type(v.astype(bf16), jnp.uint16).astype(u32)
    return ((ub << 16) | vb).reshape(u.shape[0], D_CHUNKS, LANES)


def _peer_body(idx_ref, idx_next_ref, gates_ref, h2_ref, x1_ref, fw_ref, uv_hbm, o_ref, *scratch, tt):
    bufs, (sem, w_ref, row_ref) = scratch[:PEER_NBUF], scratch[PEER_NBUF:]
    i = pl.program_id(0)
    last_step = pl.num_programs(0) - 1

    def issue(iref, tn, b, lo=0, hi=PEER_SLOTS):
        for j in range(lo, hi):
            pltpu.async_copy(uv_hbm.at[iref[tn, j]], bufs[b].at[:, j, :], sem.at[b], priority=j % 2)

    def wait_buf(b):
        pltpu.make_async_copy(uv_hbm.at[pl.ds(0, PEER_SLOTS)], bufs[b], sem.at[b]).wait()

    lane_t = lax.broadcasted_iota(jnp.int32, (PEER_SLOTS, tt), 1)
    hi_mask = jnp.asarray(0xFFFF0000, u32)

    per_chunk = PEER_SLOTS // D_CHUNKS
    chunk = lambda c: slice(c * LANES, (c + 1) * LANES)

    def act_chunk(b, hrow, c, acc):
        u = lax.bitcast_convert_type(bufs[b][c] & hi_mask, f32)
        return acc + u * hrow[:, chunk(c)]

    def finish_weights(t, acc):
        a = jnp.sum(acc, axis=1, keepdims=True)
        gate = jnp.sum(jnp.where(lane_t == t, gates_ref[...], 0.0), axis=1, keepdims=True)
        w = gate * (0.5 * a * (1.0 + lax.erf(a * (2.0 ** -0.5))))
        w_ref[...] = jnp.broadcast_to(w, (PEER_SLOTS, LANES))

    def step(t, b, iref, tn, has_next):
        b1 = (b + 1) % PEER_NBUF
        nb = (b + PEER_AHEAD) % PEER_NBUF
        if has_next:
            wait_buf(b1)
            hrow1 = h2_ref[pl.ds(t + 1, 1), :]
        wb = w_ref[...]
        xrow = x1_ref[pl.ds(t, 1), :]
        acc = jnp.zeros((PEER_SLOTS, LANES), f32)
        for c in range(D_CHUNKS):
            issue(iref, tn, nb, c * per_chunk, (c + 1) * per_chunk)
            if has_next:
                acc = act_chunk(b1, hrow1, c, acc)
            v = lax.bitcast_convert_type(bufs[b][c] << 16, f32)
            row_ref[:, chunk(c)] = xrow[:, chunk(c)] + jnp.sum(v * wb, axis=0, keepdims=True)
        if has_next:
            finish_weights(t + 1, acc)
        o_ref[pl.ds(t, 1), :] = _rms(row_ref[...], fw_ref[...], NORM_EPS)

    @pl.when(i == 0)
    def _():
        for t in range(PEER_AHEAD):
            issue(idx_ref, t, t)

    wait_buf(0)
    hrow0 = h2_ref[pl.ds(0, 1), :]
    acc0 = jnp.zeros((PEER_SLOTS, LANES), f32)
    for c in range(D_CHUNKS):
        acc0 = act_chunk(0, hrow0, c, acc0)
    finish_weights(0, acc0)

    groups = tt // PEER_NBUF

    def group(g, carry):
        for b in range(PEER_NBUF):
            t = g * PEER_NBUF + b
            step(t, b, idx_ref, t + PEER_AHEAD, True)
        return carry

    lax.fori_loop(0, groups - 1, group, 0)
    for b in range(PEER_NBUF):
        t = (groups - 1) * PEER_NBUF + b
        tn = t + PEER_AHEAD
        if tn < tt:
            step(t, b, idx_ref, tn, True)
        else:
            step(t, b, idx_next_ref, tn - tt, t + 1 < tt)

    @pl.when(i == last_step)
    def _():
        for t in range(PEER_AHEAD):
            wait_buf(t)


def _peer_apply(idx, gates_t, h2, x1, final_w, uv):
    T = x1.shape[0]
    tt = PEER_TT
    assert T % tt == 0 and tt % PEER_NBUF == 0 and tt >= 2 * PEER_NBUF
    nsteps = T // tt
    return pl.pallas_call(
        functools.partial(_peer_body, tt=tt),
        out_shape=jax.ShapeDtypeStruct((T, D_MODEL), f32),
        grid=(nsteps,),
        in_specs=[pl.BlockSpec((tt, PEER_SLOTS), lambda i: (i, 0), memory_space=pltpu.SMEM),
                  pl.BlockSpec((tt, PEER_SLOTS), lambda i: (jnp.minimum(i + 1, nsteps - 1), 0),
                               memory_space=pltpu.SMEM),
                  pl.BlockSpec((PEER_SLOTS, tt), lambda i: (0, i)),
                  pl.BlockSpec((tt, D_MODEL), lambda i: (i, 0)),
                  pl.BlockSpec((tt, D_MODEL), lambda i: (i, 0)),
                  pl.BlockSpec((1, D_MODEL), lambda i: (0, 0)),
                  pl.BlockSpec(memory_space=pl.ANY)],
        out_specs=pl.BlockSpec((tt, D_MODEL), lambda i: (i, 0)),
        scratch_shapes=[pltpu.VMEM((D_CHUNKS, PEER_SLOTS, LANES), u32) for _ in range(PEER_NBUF)]
        + [pltpu.SemaphoreType.DMA((PEER_NBUF,)),
           pltpu.VMEM((PEER_SLOTS, LANES), f32),
           pltpu.VMEM((1, D_MODEL), f32)],
        compiler_params=_params("arbitrary"),
        name="peer_apply",
    )(idx, idx, gates_t, h2, x1, final_w, uv)


def _trunk(x, rel_bias, norm1_w, w_in, lambda_q1, lambda_k1, lambda_q2, lambda_k2, subln_w,
           pool_w, pool_scale, w_out, norm2_w, peer_wq, peer_keys, peer_u, peer_v, final_norm_w):
    B, S, _ = x.shape
    T = B * S
    depth = norm1_w.shape[0]
    assert depth == 1, "the final norm is fused into the (single) layer's PEER stage"
    l = 0
    x2d = x.reshape(T, D_MODEL)
    lam_init = _lambda_init(l)
    lam = (jnp.exp(jnp.sum(lambda_q1[l].astype(f32) * lambda_k1[l].astype(f32)))
           - jnp.exp(jnp.sum(lambda_q2[l].astype(f32) * lambda_k2[l].astype(f32)))
           + lam_init).reshape(1)
    proj = _inproj(x2d, norm1_w[l].reshape(1, D_MODEL), w_in[l].astype(bf16))
    att = _attention(proj.reshape(B, S, IN_PROJ_DIM), lam, rel_bias, subln_w[l].reshape(1, ATT_V_DIM),
                     1.0 - lam_init)
    x1 = _pool_outproj(x2d, att.reshape(T, ATT_WIDTH), proj, pool_w[l].astype(bf16),
                       pool_scale[l].reshape(1, POOL_WIDTH), w_out[l].astype(bf16), S)
    keys = peer_keys[l].reshape(2 * PEER_HEADS, N_KEYS, PEER_KEY_DIM).astype(bf16)
    h2, expert_t, gates_t = _peer_route(x1, norm2_w[l].reshape(1, D_MODEL), peer_wq[l].astype(bf16), keys)
    y = _peer_apply(jnp.transpose(expert_t), gates_t, h2, x1, final_norm_w.reshape(1, D_MODEL),
                    _pack_uv(peer_u[l], peer_v[l]))
    return y.reshape(B, S, D_MODEL)


def kernel(x_prompt, x_sample, rel_bias, norm1_w, w_in, lambda_q1, lambda_k1, lambda_q2, lambda_k2, subln_w,
           pool_w, pool_scale, w_out, norm2_w, peer_wq, peer_keys, peer_u, peer_v, final_norm_w):
    args = (rel_bias, norm1_w, w_in, lambda_q1, lambda_k1, lambda_q2, lambda_k2, subln_w,
            pool_w, pool_scale, w_out, norm2_w, peer_wq, peer_keys, peer_u, peer_v, final_norm_w)
    return (_trunk(x_prompt, *args), _trunk(x_sample, *args))
```

```python
import functools
import math

import numpy as np
import jax
import jax.numpy as jnp
from jax import lax
from jax.experimental import pallas as pl
from jax.experimental.pallas import tpu as pltpu

f32 = jnp.float32
bf16 = jnp.bfloat16
u32 = jnp.uint32

D_MODEL = 2048
ATT_WIDTH = 1024
N_ATT_HEADS = 8
ATT_HEAD_DIM = 64
ATT_V_DIM = 128
POOL_WIDTH = 1024
POOL_WINDOWS = (2, 4, 8, 16)
POOL_GROUP_DIM = 256
IN_PROJ_DIM = 3 * ATT_WIDTH + POOL_WIDTH
N_BUCKETS = 32
MAX_DISTANCE = 128
N_KEYS = 128
PEER_HEADS = 8
PEER_TOPK = 16
PEER_KEY_DIM = 128
PEER_SLOTS = PEER_HEADS * PEER_TOPK
NORM_EPS = 1e-6
SUBLN_EPS = 1e-5
LOG2E = math.log2(math.e)

LANES = 128
BF16_SUBLANES = 16
VMEM_LIMIT = 56 * 1024 * 1024
D_CHUNKS = D_MODEL // LANES

ATT_TILE = 512
ATT_UNROLL = 4
INPROJ_TM, INPROJ_TN = 1024, 1024
POOL_TM = 256
ROUTE_TM = 256
ROUTE_HEADS = 4
PEER_TT = 128
PEER_NBUF = 8
PEER_AHEAD = PEER_NBUF - 1
POOL_HALO = BF16_SUBLANES


def _lambda_init(layer_idx):
    return 0.8 - 0.6 * math.exp(-0.3 * layer_idx)


def _rms(x, w, eps):
    ms = jnp.mean(x * x, axis=-1, keepdims=True)
    return x * lax.rsqrt(ms + eps) * w


def _params(*sem):
    return pltpu.CompilerParams(dimension_semantics=sem, vmem_limit_bytes=VMEM_LIMIT)


def _inproj_body(x_ref, nw_ref, w_ref, o_ref, h_ref):
    @pl.when(pl.program_id(1) == 0)
    def _():
        h_ref[...] = _rms(x_ref[...], nw_ref[...], NORM_EPS).astype(bf16)

    o_ref[...] = jnp.dot(h_ref[...], w_ref[...], preferred_element_type=f32).astype(o_ref.dtype)


def _inproj(x2d, norm_w, w_bf):
    T, N = x2d.shape[0], w_bf.shape[1]
    tm, tn = min(INPROJ_TM, T), INPROJ_TN
    return pl.pallas_call(
        _inproj_body,
        out_shape=jax.ShapeDtypeStruct((T, N), bf16),
        grid=(T // tm, N // tn),
        in_specs=[pl.BlockSpec((tm, D_MODEL), lambda i, j: (i, 0)),
                  pl.BlockSpec((1, D_MODEL), lambda i, j: (0, 0)),
                  pl.BlockSpec((D_MODEL, tn), lambda i, j: (0, j))],
        out_specs=pl.BlockSpec((tm, tn), lambda i, j: (i, j)),
        scratch_shapes=[pltpu.VMEM((tm, D_MODEL), bf16)],
        compiler_params=_params("parallel", "arbitrary"),
        name="inproj",
    )(x2d, norm_w, w_bf)


def _bucket_of_rel(rel):
    nb = N_BUCKETS // 2
    ret = (rel > 0).astype(np.int64) * nb
    n = np.abs(rel)
    max_exact = nb // 2
    large = max_exact + (np.log(np.maximum(n, 1).astype(np.float64) / max_exact)
                         / math.log(MAX_DISTANCE / max_exact) * (nb - max_exact)).astype(np.int64)
    large = np.minimum(large, nb - 1)
    return (ret + np.where(n < max_exact, n, large)).astype(np.int32)


def _attn_body(lam_ref, cfar_ref, q_ref, k_ref, v_ref, bias_ref, sw_ref, o_ref,
               qm_ref, m_ref, l_ref, acc_ref, *, lam_scale, t, nk):
    h, qi = pl.program_id(1), pl.program_id(2)
    m_ref[...] = jnp.full(m_ref.shape, -jnp.inf, f32)
    l_ref[...] = jnp.zeros(l_ref.shape, f32)
    acc_ref[...] = jnp.zeros(acc_ref.shape, f32)

    q = (q_ref[...].astype(f32) * (LOG2E * ATT_HEAD_DIM ** -0.5)).astype(bf16)
    lane = lax.broadcasted_iota(jnp.int32, q.shape, 1)
    qm_ref[:t] = jnp.where(lane < ATT_HEAD_DIM, q, jnp.zeros_like(q))
    qm_ref[t:] = jnp.where(lane >= ATT_HEAD_DIM, q, jnp.zeros_like(q))
    nt = (((1,), (1,)), ((), ()))

    def chunk(j, tile_d, c):
        start = pl.multiple_of(j * t, t)
        k = k_ref[pl.ds(start, t), :]
        v = v_ref[pl.ds(start, t), :]
        s = lax.dot_general(qm_ref[...], k, nt, preferred_element_type=f32)
        m_prev = m_ref[...]
        if tile_d is None:
            m_new = jnp.maximum(m_prev, jnp.max(s, axis=1, keepdims=True) + c)
            shift = m_new - c
        else:
            b = bias_ref[tile_d + 1]
            s = jnp.concatenate([s[:t] + b, s[t:] + b], axis=0)
            m_new = jnp.maximum(m_prev, jnp.max(s, axis=1, keepdims=True))
            shift = m_new
        alpha = jnp.exp2(m_prev - m_new)
        p = jnp.exp2(s - jnp.concatenate([shift] * (t // LANES), axis=1))
        l_ref[...] = alpha * l_ref[...] + jnp.sum(p, axis=1, keepdims=True)
        acc_ref[...] = alpha * acc_ref[...] + jnp.dot(p.astype(bf16), v, preferred_element_type=f32)
        m_ref[...] = m_new

    def far_sweep(lo, hi, c):
        groups = (hi - lo) // ATT_UNROLL

        def body(j, carry):
            for n in range(ATT_UNROLL):
                chunk(lo + ATT_UNROLL * j + n, None, c)
            return carry
        lax.fori_loop(0, groups, body, 0)

        done = lo + ATT_UNROLL * groups
        left = hi - done

        @pl.when(left >= 2)
        def _():
            chunk(done, None, c)
            chunk(done + 1, None, c)

        @pl.when(left % 2 == 1)
        def _():
            chunk(hi - 1, None, c)

    far_sweep(0, jnp.maximum(qi - 1, 0), cfar_ref[h, 0])
    for d in (-1, 0, 1):
        j = qi + d

        @pl.when(jnp.logical_and(j >= 0, j < nk))
        def _():
            chunk(j, d, None)
    far_sweep(jnp.minimum(qi + 2, nk), nk, cfar_ref[h, 1])

    lam = lam_ref[0]
    out = acc_ref[:t] / l_ref[:t] - lam * (acc_ref[t:] / l_ref[t:])
    o_ref[...] = (_rms(out, sw_ref[...], SUBLN_EPS) * lam_scale).astype(o_ref.dtype)


def _bias_tiles(rel_bias, t):
    span = 2 * t - 1
    rel_vec = np.arange(-span, span + 1)
    bias_vec = jnp.transpose(rel_bias[jnp.asarray(_bucket_of_rel(rel_vec))].astype(f32))
    tiles = []
    for d in (-1, 0, 1):
        w = lax.slice_in_dim(bias_vec, t + d * t, t + d * t + span, axis=1)
        wp = jnp.pad(w, ((0, 0), (0, 1)))
        skew = jnp.tile(wp, (1, t))[:, :t * span].reshape(-1, t, span)
        tiles.append(skew[:, :, t - 1:t - 1 + t])
    return jnp.stack(tiles, axis=1)


def _attention(proj3, lam, rel_bias, subln_w, lam_scale):
    B, S, _ = proj3.shape
    t = min(ATT_TILE, S)
    assert S % t == 0 and t >= MAX_DISTANCE
    bias_tiles = _bias_tiles(rel_bias, t) * LOG2E
    far = _bucket_of_rel(np.array([-MAX_DISTANCE, MAX_DISTANCE]))
    cfar = jnp.transpose(rel_bias[jnp.asarray(far)].astype(f32)) * LOG2E
    n_heads = N_ATT_HEADS
    nk = S // t
    return pl.pallas_call(
        functools.partial(_attn_body, lam_scale=lam_scale, t=t, nk=nk),
        out_shape=jax.ShapeDtypeStruct((B, S, ATT_WIDTH), bf16),
        grid=(B, n_heads, nk),
        in_specs=[pl.BlockSpec(memory_space=pltpu.SMEM),
                  pl.BlockSpec(memory_space=pltpu.SMEM),
                  pl.BlockSpec((None, t, LANES), lambda b, h, qi: (b, qi, h)),
                  pl.BlockSpec((None, S, LANES), lambda b, h, qi: (b, 0, n_heads + h)),
                  pl.BlockSpec((None, S, LANES), lambda b, h, qi: (b, 0, 2 * n_heads + h)),
                  pl.BlockSpec((None, 3, t, t), lambda b, h, qi: (h, 0, 0, 0)),
                  pl.BlockSpec((1, ATT_V_DIM), lambda b, h, qi: (0, 0))],
        out_specs=pl.BlockSpec((None, t, LANES), lambda b, h, qi: (b, qi, h)),
        scratch_shapes=[pltpu.VMEM((2 * t, LANES), bf16),
                        pltpu.VMEM((2 * t, LANES), f32), pltpu.VMEM((2 * t, LANES), f32),
                        pltpu.VMEM((2 * t, ATT_V_DIM), f32)],
        compiler_params=_params("parallel", "parallel", "arbitrary"),
        name="diff_attention",
    )(lam, cfar, proj3, proj3, proj3, bias_tiles, subln_w)


def _pool_body(x_ref, att_ref, p_ref, prev_ref, next_ref, pw_ref, ps_ref, wo_ref, o_ref, *, seq, tm):
    i = pl.program_id(0)
    t0 = (i * tm) % seq
    prev_ok = (t0 > 0).astype(f32)
    next_ok = (t0 + tm < seq).astype(f32)
    pm = p_ref[...].astype(f32)
    ext = jnp.concatenate([prev_ref[...].astype(f32) * prev_ok, pm,
                           next_ref[...].astype(f32) * next_ok], axis=0)
    pos = t0 + lax.broadcasted_iota(jnp.int32, (tm, 1), 0)
    pooled = []
    for g, w in enumerate(POOL_WINDOWS):
        cs = slice(g * POOL_GROUP_DIM, (g + 1) * POOL_GROUP_DIM)
        arr = ext[:, cs]
        n = 1
        while n < w:
            rows = arr.shape[0]
            arr = arr[0:rows - n] + arr[n:rows]
            n *= 2
        half = w // 2
        wsum = arr[POOL_HALO - half:POOL_HALO - half + tm]
        count = (jnp.minimum(pos + half, seq) - jnp.maximum(pos - half, 0)).astype(f32)
        mixed = (wsum / count - pm[:, cs]).astype(bf16)
        pooled.append((jnp.dot(mixed, pw_ref[g], preferred_element_type=f32) * ps_ref[:, cs]).astype(bf16))
    cat = jnp.concatenate([att_ref[...]] + pooled, axis=1)
    o_ref[...] = x_ref[...] + jnp.dot(cat, wo_ref[...], preferred_element_type=f32)


def _pool_outproj(x2d, att2d, proj, pool_w_bf, pool_scale, w_out_bf, seq):
    T = x2d.shape[0]
    tm = min(POOL_TM, seq)
    assert seq % tm == 0 and tm % POOL_HALO == 0
    hb = tm // POOL_HALO
    last = T // POOL_HALO - 1
    pcol = 3 * ATT_WIDTH // POOL_WIDTH
    return pl.pallas_call(
        functools.partial(_pool_body, seq=seq, tm=tm),
        out_shape=jax.ShapeDtypeStruct((T, D_MODEL), f32),
        grid=(T // tm,),
        in_specs=[pl.BlockSpec((tm, D_MODEL), lambda i: (i, 0)),
                  pl.BlockSpec((tm, ATT_WIDTH), lambda i: (i, 0)),
                  pl.BlockSpec((tm, POOL_WIDTH), lambda i: (i, pcol)),
                  pl.BlockSpec((POOL_HALO, POOL_WIDTH), lambda i: (jnp.maximum(i * hb - 1, 0), pcol)),
                  pl.BlockSpec((POOL_HALO, POOL_WIDTH), lambda i: (jnp.minimum((i + 1) * hb, last), pcol)),
                  pl.BlockSpec((len(POOL_WINDOWS), POOL_GROUP_DIM, POOL_GROUP_DIM), lambda i: (0, 0, 0)),
                  pl.BlockSpec((1, POOL_WIDTH), lambda i: (0, 0)),
                  pl.BlockSpec((D_MODEL, D_MODEL), lambda i: (0, 0))],
        out_specs=pl.BlockSpec((tm, D_MODEL), lambda i: (i, 0)),
        compiler_params=_params("parallel"),
        name="pool_outproj",
    )(x2d, att2d, proj, proj, proj, pool_w_bf, pool_scale, w_out_bf)


def _topk_rows(s, k, ids=None, id_bound=None):
    return _topk_rows_lockstep([s], k, ids, id_bound)[0]


def _topk_rows_lockstep(arrays, k, ids=None, id_bound=None):
    if ids is None:
        ids = lax.broadcasted_iota(jnp.int32, arrays[0].shape, 0)
        id_bound = arrays[0].shape[0]
    arrays = list(arrays)
    vals = [[] for _ in arrays]
    picked = [[] for _ in arrays]
    for _ in range(k):
        for n, s in enumerate(arrays):
            m = jnp.max(s, axis=0, keepdims=True)
            pick = jnp.minimum(jnp.min(jnp.where(s == m, ids, id_bound), axis=0, keepdims=True), id_bound - 1)
            vals[n].append(m)
            picked[n].append(pick)
            arrays[n] = jnp.where(ids == pick, -jnp.inf, s)
    return [(jnp.concatenate(v, axis=0), jnp.concatenate(p, axis=0)) for v, p in zip(vals, picked)]


def _candidate_grid(s1, s2):
    k, hk = PEER_TOPK, PEER_TOPK // 2
    m = s1.shape[1]
    iota_k = lax.broadcasted_iota(jnp.int32, (k, m), 0)
    iota_h = lax.broadcasted_iota(jnp.int32, (hk, m), 0)
    vals = [s1[0:1, :] + s2]
    ids = [iota_k]
    for a in range(1, hk):
        vals.append(s1[a:a + 1, :] + s2[0:hk, :])
        ids.append(iota_h + a * k)
    vals.append(s1[hk:k, :] + s2[0:1, :])
    ids.append((iota_h + hk) * k)
    return jnp.concatenate(vals, axis=0), jnp.concatenate(ids, axis=0)


def _select_rows(table, sel):
    out = jnp.zeros(sel.shape, table.dtype)
    for a in range(table.shape[0]):
        out = jnp.where(sel == a, table[a:a + 1, :], out)
    return out


def _route_body(x1_ref, nw_ref, wq_ref, keys_ref, h2_ref, e_ref, g_ref, qp_ref):
    h = pl.program_id(1)

    @pl.when(h == 0)
    def _():
        h2 = _rms(x1_ref[...], nw_ref[...], NORM_EPS)
        h2_ref[...] = h2
        qp = jnp.dot(h2.astype(bf16), wq_ref[...], preferred_element_type=f32).astype(bf16)
        for hh in range(PEER_HEADS):
            qp_ref[hh] = qp[:, hh * 2 * PEER_KEY_DIM:(hh + 1) * 2 * PEER_KEY_DIM]

    nt = (((1,), (1,)), ((), ()))
    tm = qp_ref.shape[1]
    for c0 in range(0, tm, LANES):
        cols = slice(c0, c0 + LANES)
        scores = []
        for g in range(ROUTE_HEADS):
            q = qp_ref[h * ROUTE_HEADS + g, cols, :]
            for i in range(2):
                scores.append(lax.dot_general(
                    keys_ref[2 * (h * ROUTE_HEADS + g) + i], q[:, i * PEER_KEY_DIM:(i + 1) * PEER_KEY_DIM], nt,
                    preferred_element_type=f32))
        halves = _topk_rows_lockstep(scores, PEER_TOPK)
        grids = [_candidate_grid(halves[2 * g][0], halves[2 * g + 1][0]) for g in range(ROUTE_HEADS)]
        flat_ids = grids[0][1]
        tops = _topk_rows_lockstep([c for c, _ in grids], PEER_TOPK, flat_ids, PEER_TOPK * PEER_TOPK)
        for g, (best, flat) in enumerate(tops):
            i1, i2 = halves[2 * g][1], halves[2 * g + 1][1]
            e1 = _select_rows(i1, lax.shift_right_logical(flat, int(math.log2(PEER_TOPK))))
            e2 = _select_rows(i2, lax.bitwise_and(flat, PEER_TOPK - 1))
            rows = slice(g * PEER_TOPK, (g + 1) * PEER_TOPK)
            e_ref[rows, cols] = e1 * N_KEYS + e2
            ex = jnp.exp(best - jnp.max(best, axis=0, keepdims=True))
            g_ref[rows, cols] = ex / jnp.sum(ex, axis=0, keepdims=True)


def _peer_route(x1, norm_w, wq_bf, keys_bf):
    T = x1.shape[0]
    tm = min(ROUTE_TM, T)
    return pl.pallas_call(
        _route_body,
        out_shape=(jax.ShapeDtypeStruct((T, D_MODEL), f32),
                   jax.ShapeDtypeStruct((PEER_SLOTS, T), jnp.int32),
                   jax.ShapeDtypeStruct((PEER_SLOTS, T), f32)),
        grid=(T // tm, PEER_HEADS // ROUTE_HEADS),
        in_specs=[pl.BlockSpec((tm, D_MODEL), lambda i, h: (i, 0)),
                  pl.BlockSpec((1, D_MODEL), lambda i, h: (0, 0)),
                  pl.BlockSpec((D_MODEL, 2 * PEER_KEY_DIM * PEER_HEADS), lambda i, h: (0, 0)),
                  pl.BlockSpec((2 * PEER_HEADS, N_KEYS, PEER_KEY_DIM), lambda i, h: (0, 0, 0))],
        out_specs=(pl.BlockSpec((tm, D_MODEL), lambda i, h: (i, 0)),
                   pl.BlockSpec((ROUTE_HEADS * PEER_TOPK, tm), lambda i, h: (h, i)),
                   pl.BlockSpec((ROUTE_HEADS * PEER_TOPK, tm), lambda i, h: (h, i))),
        scratch_shapes=[pltpu.VMEM((PEER_HEADS, tm, 2 * PEER_KEY_DIM), bf16)],
        compiler_params=_params("parallel", "arbitrary"),
        name="peer_route",
    )(x1, norm_w, wq_bf, keys_bf)


def _pack_uv(u, v):
    ub = lax.bitcast_convert_type(u.astype(bf16), jnp.uint16).astype(u32)
    vb = lax.bitcast_convert_type(v.astype(bf16), jnp.uint16).astype(u32)
    return ((ub << 16) | vb).reshape(u.shape[0], D_CHUNKS, LANES)


def _peer_body(idx_ref, idx_next_ref, gates_ref, h2_ref, x1_ref, fw_ref, uv_hbm, o_ref, *scratch, tt):
    bufs, (sem, w_ref, row_ref) = scratch[:PEER_NBUF], scratch[PEER_NBUF:]
    i = pl.program_id(0)
    last_step = pl.num_programs(0) - 1

    def issue(iref, tn, b, lo=0, hi=PEER_SLOTS):
        for j in range(lo, hi):
            pltpu.async_copy(uv_hbm.at[iref[tn, j]], bufs[b].at[:, j, :], sem.at[b], priority=j % 2)

    def wait_buf(b):
        pltpu.make_async_copy(uv_hbm.at[pl.ds(0, PEER_SLOTS)], bufs[b], sem.at[b]).wait()

    lane_t = lax.broadcasted_iota(jnp.int32, (PEER_SLOTS, tt), 1)
    hi_mask = jnp.asarray(0xFFFF0000, u32)

    per_chunk = PEER_SLOTS // D_CHUNKS
    chunk = lambda c: slice(c * LANES, (c + 1) * LANES)

    def act_chunk(b, hrow, c, acc):
        u = lax.bitcast_convert_type(bufs[b][c] & hi_mask, f32)
        return acc + u * hrow[:, chunk(c)]

    def finish_weights(t, acc):
        a = jnp.sum(acc, axis=1, keepdims=True)
        gate = jnp.sum(jnp.where(lane_t == t, gates_ref[...], 0.0), axis=1, keepdims=True)
        w = gate * (0.5 * a * (1.0 + lax.erf(a * (2.0 ** -0.5))))
        w_ref[...] = jnp.broadcast_to(w, (PEER_SLOTS, LANES))

    def step(t, b, iref, tn, has_next):
        b1 = (b + 1) % PEER_NBUF
        nb = (b + PEER_AHEAD) % PEER_NBUF
        if has_next:
            wait_buf(b1)
            hrow1 = h2_ref[pl.ds(t + 1, 1), :]
        wb = w_ref[...]
        xrow = x1_ref[pl.ds(t, 1), :]
        acc = jnp.zeros((PEER_SLOTS, LANES), f32)
        for c in range(D_CHUNKS):
            issue(iref, tn, nb, c * per_chunk, (c + 1) * per_chunk)
            if has_next:
                acc = act_chunk(b1, hrow1, c, acc)
            v = lax.bitcast_convert_type(bufs[b][c] << 16, f32)
            row_ref[:, chunk(c)] = xrow[:, chunk(c)] + jnp.sum(v * wb, axis=0, keepdims=True)
        if has_next:
            finish_weights(t + 1, acc)
        o_ref[pl.ds(t, 1), :] = _rms(row_ref[...], fw_ref[...], NORM_EPS)

    @pl.when(i == 0)
    def _():
        for t in range(PEER_AHEAD):
            issue(idx_ref, t, t)

    wait_buf(0)
    hrow0 = h2_ref[pl.ds(0, 1), :]
    acc0 = jnp.zeros((PEER_SLOTS, LANES), f32)
    for c in range(D_CHUNKS):
        acc0 = act_chunk(0, hrow0, c, acc0)
    finish_weights(0, acc0)

    groups = tt // PEER_NBUF

    def group(g, carry):
        for b in range(PEER_NBUF):
            t = g * PEER_NBUF + b
            step(t, b, idx_ref, t + PEER_AHEAD, True)
        return carry

    lax.fori_loop(0, groups - 1, group, 0)
    for b in range(PEER_NBUF):
        t = (groups - 1) * PEER_NBUF + b
        tn = t + PEER_AHEAD
        if tn < tt:
            step(t, b, idx_ref, tn, True)
        else:
            step(t, b, idx_next_ref, tn - tt, t + 1 < tt)

    @pl.when(i == last_step)
    def _():
        for t in range(PEER_AHEAD):
            wait_buf(t)


def _peer_apply(idx, gates_t, h2, x1, final_w, uv):
    T = x1.shape[0]
    tt = PEER_TT
    assert T % tt == 0 and tt % PEER_NBUF == 0 and tt >= 2 * PEER_NBUF
    nsteps = T // tt
    return pl.pallas_call(
        functools.partial(_peer_body, tt=tt),
        out_shape=jax.ShapeDtypeStruct((T, D_MODEL), f32),
        grid=(nsteps,),
        in_specs=[pl.BlockSpec((tt, PEER_SLOTS), lambda i: (i, 0), memory_space=pltpu.SMEM),
                  pl.BlockSpec((tt, PEER_SLOTS), lambda i: (jnp.minimum(i + 1, nsteps - 1), 0),
                               memory_space=pltpu.SMEM),
                  pl.BlockSpec((PEER_SLOTS, tt), lambda i: (0, i)),
                  pl.BlockSpec((tt, D_MODEL), lambda i: (i, 0)),
                  pl.BlockSpec((tt, D_MODEL), lambda i: (i, 0)),
                  pl.BlockSpec((1, D_MODEL), lambda i: (0, 0)),
                  pl.BlockSpec(memory_space=pl.ANY)],
        out_specs=pl.BlockSpec((tt, D_MODEL), lambda i: (i, 0)),
        scratch_shapes=[pltpu.VMEM((D_CHUNKS, PEER_SLOTS, LANES), u32) for _ in range(PEER_NBUF)]
        + [pltpu.SemaphoreType.DMA((PEER_NBUF,)),
           pltpu.VMEM((PEER_SLOTS, LANES), f32),
           pltpu.VMEM((1, D_MODEL), f32)],
        compiler_params=_params("arbitrary"),
        name="peer_apply",
    )(idx, idx, gates_t, h2, x1, final_w, uv)


def _trunk(x, rel_bias, norm1_w, w_in, lambda_q1, lambda_k1, lambda_q2, lambda_k2, subln_w,
           pool_w, pool_scale, w_out, norm2_w, peer_wq, peer_keys, peer_u, peer_v, final_norm_w):
    B, S, _ = x.shape
    T = B * S
    depth = norm1_w.shape[0]
    assert depth == 1, "the final norm is fused into the (single) layer's PEER stage"
    l = 0
    x2d = x.reshape(T, D_MODEL)
    lam_init = _lambda_init(l)
    lam = (jnp.exp(jnp.sum(lambda_q1[l].astype(f32) * lambda_k1[l].astype(f32)))
           - jnp.exp(jnp.sum(lambda_q2[l].astype(f32) * lambda_k2[l].astype(f32)))
           + lam_init).reshape(1)
    proj = _inproj(x2d, norm1_w[l].reshape(1, D_MODEL), w_in[l].astype(bf16))
    att = _attention(proj.reshape(B, S, IN_PROJ_DIM), lam, rel_bias, subln_w[l].reshape(1, ATT_V_DIM),
                     1.0 - lam_init)
    x1 = _pool_outproj(x2d, att.reshape(T, ATT_WIDTH), proj, pool_w[l].astype(bf16),
                       pool_scale[l].reshape(1, POOL_WIDTH), w_out[l].astype(bf16), S)
    keys = peer_keys[l].reshape(2 * PEER_HEADS, N_KEYS, PEER_KEY_DIM).astype(bf16)
    h2, expert_t, gates_t = _peer_route(x1, norm2_w[l].reshape(1, D_MODEL), peer_wq[l].astype(bf16), keys)
    y = _peer_apply(jnp.transpose(expert_t), gates_t, h2, x1, final_norm_w.reshape(1, D_MODEL),
                    _pack_uv(peer_u[l], peer_v[l]))
    return y.reshape(B, S, D_MODEL)


def kernel(x_prompt, x_sample, rel_bias, norm1_w, w_in, lambda_q1, lambda_k1, lambda_q2, lambda_k2, subln_w,
           pool_w, pool_scale, w_out, norm2_w, peer_wq, peer_keys, peer_u, peer_v, final_norm_w):
    args = (rel_bias, norm1_w, w_in, lambda_q1, lambda_k1, lambda_q2, lambda_k2, subln_w,
            pool_w, pool_scale, w_out, norm2_w, peer_wq, peer_keys, peer_u, peer_v, final_norm_w)
    return (_trunk(x_prompt, *args), _trunk(x_sample, *args))
```

```python
import functools
import math

import numpy as np
import jax
import jax.numpy as jnp
from jax import lax
from jax.experimental import pallas as pl
from jax.experimental.pallas import tpu as pltpu

f32 = jnp.float32
bf16 = jnp.bfloat16
u32 = jnp.uint32

D_MODEL = 2048
ATT_WIDTH = 1024
N_ATT_HEADS = 8
ATT_HEAD_DIM = 64
ATT_V_DIM = 128
POOL_WIDTH = 1024
POOL_WINDOWS = (2, 4, 8, 16)
POOL_GROUP_DIM = 256
IN_PROJ_DIM = 3 * ATT_WIDTH + POOL_WIDTH
N_BUCKETS = 32
MAX_DISTANCE = 128
N_KEYS = 128
PEER_HEADS = 8
PEER_TOPK = 16
PEER_KEY_DIM = 128
PEER_SLOTS = PEER_HEADS * PEER_TOPK
NORM_EPS = 1e-6
SUBLN_EPS = 1e-5
LOG2E = math.log2(math.e)

LANES = 128
BF16_SUBLANES = 16
VMEM_LIMIT = 56 * 1024 * 1024
D_CHUNKS = D_MODEL // LANES

ATT_TILE = 512
ATT_UNROLL = 4
INPROJ_TM, INPROJ_TN = 1024, 1024
POOL_TM = 256
ROUTE_TM = 256
ROUTE_HEADS = 4
PEER_TT = 256
PEER_NBUF = 8
PEER_AHEAD = PEER_NBUF - 1
POOL_HALO = BF16_SUBLANES


def _lambda_init(layer_idx):
    return 0.8 - 0.6 * math.exp(-0.3 * layer_idx)


def _rms(x, w, eps):
    ms = jnp.mean(x * x, axis=-1, keepdims=True)
    return x * lax.rsqrt(ms + eps) * w


def _params(*sem):
    return pltpu.CompilerParams(dimension_semantics=sem, vmem_limit_bytes=VMEM_LIMIT)


def _inproj_body(x_ref, nw_ref, w_ref, o_ref, h_ref):
    @pl.when(pl.program_id(1) == 0)
    def _():
        h_ref[...] = _rms(x_ref[...], nw_ref[...], NORM_EPS).astype(bf16)

    o_ref[...] = jnp.dot(h_ref[...], w_ref[...], preferred_element_type=f32).astype(o_ref.dtype)


def _inproj(x2d, norm_w, w_bf):
    T, N = x2d.shape[0], w_bf.shape[1]
    tm, tn = min(INPROJ_TM, T), INPROJ_TN
    return pl.pallas_call(
        _inproj_body,
        out_shape=jax.ShapeDtypeStruct((T, N), bf16),
        grid=(T // tm, N // tn),
        in_specs=[pl.BlockSpec((tm, D_MODEL), lambda i, j: (i, 0)),
                  pl.BlockSpec((1, D_MODEL), lambda i, j: (0, 0)),
                  pl.BlockSpec((D_MODEL, tn), lambda i, j: (0, j))],
        out_specs=pl.BlockSpec((tm, tn), lambda i, j: (i, j)),
        scratch_shapes=[pltpu.VMEM((tm, D_MODEL), bf16)],
        compiler_params=_params("parallel", "arbitrary"),
        name="inproj",
    )(x2d, norm_w, w_bf)


def _bucket_of_rel(rel):
    nb = N_BUCKETS // 2
    ret = (rel > 0).astype(np.int64) * nb
    n = np.abs(rel)
    max_exact = nb // 2
    large = max_exact + (np.log(np.maximum(n, 1).astype(np.float64) / max_exact)
                         / math.log(MAX_DISTANCE / max_exact) * (nb - max_exact)).astype(np.int64)
    large = np.minimum(large, nb - 1)
    return (ret + np.where(n < max_exact, n, large)).astype(np.int32)


def _attn_body(lam_ref, cfar_ref, q_ref, k_ref, v_ref, bias_ref, sw_ref, o_ref,
               qm_ref, m_ref, l_ref, acc_ref, *, lam_scale, t, nk):
    h, qi = pl.program_id(1), pl.program_id(2)
    m_ref[...] = jnp.full(m_ref.shape, -jnp.inf, f32)
    l_ref[...] = jnp.zeros(l_ref.shape, f32)
    acc_ref[...] = jnp.zeros(acc_ref.shape, f32)

    q = (q_ref[...].astype(f32) * (LOG2E * ATT_HEAD_DIM ** -0.5)).astype(bf16)
    lane = lax.broadcasted_iota(jnp.int32, q.shape, 1)
    qm_ref[:t] = jnp.where(lane < ATT_HEAD_DIM, q, jnp.zeros_like(q))
    qm_ref[t:] = jnp.where(lane >= ATT_HEAD_DIM, q, jnp.zeros_like(q))
    nt = (((1,), (1,)), ((), ()))

    def chunk(j, tile_d, c):
        start = pl.multiple_of(j * t, t)
        k = k_ref[pl.ds(start, t), :]
        v = v_ref[pl.ds(start, t), :]
        s = lax.dot_general(qm_ref[...], k, nt, preferred_element_type=f32)
        m_prev = m_ref[...]
        if tile_d is None:
            m_new = jnp.maximum(m_prev, jnp.max(s, axis=1, keepdims=True) + c)
            shift = m_new - c
        else:
            b = bias_ref[tile_d + 1]
            s = jnp.concatenate([s[:t] + b, s[t:] + b], axis=0)
            m_new = jnp.maximum(m_prev, jnp.max(s, axis=1, keepdims=True))
            shift = m_new
        alpha = jnp.exp2(m_prev - m_new)
        p = jnp.exp2(s - jnp.concatenate([shift] * (t // LANES), axis=1))
        l_ref[...] = alpha * l_ref[...] + jnp.sum(p, axis=1, keepdims=True)
        acc_ref[...] = alpha * acc_ref[...] + jnp.dot(p.astype(bf16), v, preferred_element_type=f32)
        m_ref[...] = m_new

    def far_sweep(lo, hi, c):
        groups = (hi - lo) // ATT_UNROLL

        def body(j, carry):
            for n in range(ATT_UNROLL):
                chunk(lo + ATT_UNROLL * j + n, None, c)
            return carry
        lax.fori_loop(0, groups, body, 0)

        done = lo + ATT_UNROLL * groups
        left = hi - done
        width = ATT_UNROLL // 2
        while width >= 1:
            @pl.when((left & width) != 0)
            def _(done=done, width=width):
                for n in range(width):
                    chunk(done + n, None, c)
            done = done + (left & width)
            width //= 2

    far_sweep(0, jnp.maximum(qi - 1, 0), cfar_ref[h, 0])
    for d in (-1, 0, 1):
        j = qi + d

        @pl.when(jnp.logical_and(j >= 0, j < nk))
        def _():
            chunk(j, d, None)
    far_sweep(jnp.minimum(qi + 2, nk), nk, cfar_ref[h, 1])

    lam = lam_ref[0]
    out = acc_ref[:t] / l_ref[:t] - lam * (acc_ref[t:] / l_ref[t:])
    o_ref[...] = (_rms(out, sw_ref[...], SUBLN_EPS) * lam_scale).astype(o_ref.dtype)


def _bias_tiles(rel_bias, t):
    span = 2 * t - 1
    rel_vec = np.arange(-span, span + 1)
    bias_vec = jnp.transpose(rel_bias[jnp.asarray(_bucket_of_rel(rel_vec))].astype(f32))
    tiles = []
    for d in (-1, 0, 1):
        w = lax.slice_in_dim(bias_vec, t + d * t, t + d * t + span, axis=1)
        wp = jnp.pad(w, ((0, 0), (0, 1)))
        skew = jnp.tile(wp, (1, t))[:, :t * span].reshape(-1, t, span)
        tiles.append(skew[:, :, t - 1:t - 1 + t])
    return jnp.stack(tiles, axis=1)


def _attention(proj3, lam, rel_bias, subln_w, lam_scale):
    B, S, _ = proj3.shape
    t = min(ATT_TILE, S)
    assert S % t == 0 and t >= MAX_DISTANCE
    bias_tiles = _bias_tiles(rel_bias, t) * LOG2E
    far = _bucket_of_rel(np.array([-MAX_DISTANCE, MAX_DISTANCE]))
    cfar = jnp.transpose(rel_bias[jnp.asarray(far)].astype(f32)) * LOG2E
    n_heads = N_ATT_HEADS
    nk = S // t
    return pl.pallas_call(
        functools.partial(_attn_body, lam_scale=lam_scale, t=t, nk=nk),
        out_shape=jax.ShapeDtypeStruct((B, S, ATT_WIDTH), bf16),
        grid=(B, n_heads, nk),
        in_specs=[pl.BlockSpec(memory_space=pltpu.SMEM),
                  pl.BlockSpec(memory_space=pltpu.SMEM),
                  pl.BlockSpec((None, t, LANES), lambda b, h, qi: (b, qi, h)),
                  pl.BlockSpec((None, S, LANES), lambda b, h, qi: (b, 0, n_heads + h)),
                  pl.BlockSpec((None, S, LANES), lambda b, h, qi: (b, 0, 2 * n_heads + h)),
                  pl.BlockSpec((None, 3, t, t), lambda b, h, qi: (h, 0, 0, 0)),
                  pl.BlockSpec((1, ATT_V_DIM), lambda b, h, qi: (0, 0))],
        out_specs=pl.BlockSpec((None, t, LANES), lambda b, h, qi: (b, qi, h)),
        scratch_shapes=[pltpu.VMEM((2 * t, LANES), bf16),
                        pltpu.VMEM((2 * t, LANES), f32), pltpu.VMEM((2 * t, LANES), f32),
                        pltpu.VMEM((2 * t, ATT_V_DIM), f32)],
        compiler_params=_params("parallel", "parallel", "arbitrary"),
        name="diff_attention",
    )(lam, cfar, proj3, proj3, proj3, bias_tiles, subln_w)


def _pool_body(x_ref, att_ref, p_ref, prev_ref, next_ref, pw_ref, ps_ref, wo_ref, o_ref, *, seq, tm):
    i = pl.program_id(0)
    t0 = (i * tm) % seq
    prev_ok = (t0 > 0).astype(f32)
    next_ok = (t0 + tm < seq).astype(f32)
    pm = p_ref[...].astype(f32)
    ext = jnp.concatenate([prev_ref[...].astype(f32) * prev_ok, pm,
                           next_ref[...].astype(f32) * next_ok], axis=0)
    pos = t0 + lax.broadcasted_iota(jnp.int32, (tm, 1), 0)
    pooled = []
    for g, w in enumerate(POOL_WINDOWS):
        cs = slice(g * POOL_GROUP_DIM, (g + 1) * POOL_GROUP_DIM)
        arr = ext[:, cs]
        n = 1
        while n < w:
            rows = arr.shape[0]
            arr = arr[0:rows - n] + arr[n:rows]
            n *= 2
        half = w // 2
        wsum = arr[POOL_HALO - half:POOL_HALO - half + tm]
        count = (jnp.minimum(pos + half, seq) - jnp.maximum(pos - half, 0)).astype(f32)
        mixed = (wsum / count - pm[:, cs]).astype(bf16)
        pooled.append((jnp.dot(mixed, pw_ref[g], preferred_element_type=f32) * ps_ref[:, cs]).astype(bf16))
    cat = jnp.concatenate([att_ref[...]] + pooled, axis=1)
    o_ref[...] = x_ref[...] + jnp.dot(cat, wo_ref[...], preferred_element_type=f32)


def _pool_outproj(x2d, att2d, proj, pool_w_bf, pool_scale, w_out_bf, seq):
    T = x2d.shape[0]
    tm = min(POOL_TM, seq)
    assert seq % tm == 0 and tm % POOL_HALO == 0
    hb = tm // POOL_HALO
    last = T // POOL_HALO - 1
    pcol = 3 * ATT_WIDTH // POOL_WIDTH
    return pl.pallas_call(
        functools.partial(_pool_body, seq=seq, tm=tm),
        out_shape=jax.ShapeDtypeStruct((T, D_MODEL), f32),
        grid=(T // tm,),
        in_specs=[pl.BlockSpec((tm, D_MODEL), lambda i: (i, 0)),
                  pl.BlockSpec((tm, ATT_WIDTH), lambda i: (i, 0)),
                  pl.BlockSpec((tm, POOL_WIDTH), lambda i: (i, pcol)),
                  pl.BlockSpec((POOL_HALO, POOL_WIDTH), lambda i: (jnp.maximum(i * hb - 1, 0), pcol)),
                  pl.BlockSpec((POOL_HALO, POOL_WIDTH), lambda i: (jnp.minimum((i + 1) * hb, last), pcol)),
                  pl.BlockSpec((len(POOL_WINDOWS), POOL_GROUP_DIM, POOL_GROUP_DIM), lambda i: (0, 0, 0)),
                  pl.BlockSpec((1, POOL_WIDTH), lambda i: (0, 0)),
                  pl.BlockSpec((D_MODEL, D_MODEL), lambda i: (0, 0))],
        out_specs=pl.BlockSpec((tm, D_MODEL), lambda i: (i, 0)),
        compiler_params=_params("parallel"),
        name="pool_outproj",
    )(x2d, att2d, proj, proj, proj, pool_w_bf, pool_scale, w_out_bf)


def _topk_rows(s, k, ids=None, id_bound=None):
    return _topk_rows_lockstep([s], k, ids, id_bound)[0]


def _topk_rows_lockstep(arrays, k, ids=None, id_bound=None):
    if ids is None:
        ids = lax.broadcasted_iota(jnp.int32, arrays[0].shape, 0)
        id_bound = arrays[0].shape[0]
    arrays = list(arrays)
    vals = [[] for _ in arrays]
    picked = [[] for _ in arrays]
    for _ in range(k):
        for n, s in enumerate(arrays):
            m = jnp.max(s, axis=0, keepdims=True)
            pick = jnp.minimum(jnp.min(jnp.where(s == m, ids, id_bound), axis=0, keepdims=True), id_bound - 1)
            vals[n].append(m)
            picked[n].append(pick)
            arrays[n] = jnp.where(ids == pick, -jnp.inf, s)
    return [(jnp.concatenate(v, axis=0), jnp.concatenate(p, axis=0)) for v, p in zip(vals, picked)]


def _candidate_grid(s1, s2):
    k, hk = PEER_TOPK, PEER_TOPK // 2
    m = s1.shape[1]
    iota_k = lax.broadcasted_iota(jnp.int32, (k, m), 0)
    iota_h = lax.broadcasted_iota(jnp.int32, (hk, m), 0)
    vals = [s1[0:1, :] + s2]
    ids = [iota_k]
    for a in range(1, hk):
        vals.append(s1[a:a + 1, :] + s2[0:hk, :])
        ids.append(iota_h + a * k)
    vals.append(s1[hk:k, :] + s2[0:1, :])
    ids.append((iota_h + hk) * k)
    return jnp.concatenate(vals, axis=0), jnp.concatenate(ids, axis=0)


def _select_rows(table, sel):
    out = jnp.zeros(sel.shape, table.dtype)
    for a in range(table.shape[0]):
        out = jnp.where(sel == a, table[a:a + 1, :], out)
    return out


def _route_body(x1_ref, nw_ref, wq_ref, keys_ref, h2_ref, e_ref, g_ref, qp_ref):
    h = pl.program_id(1)

    @pl.when(h == 0)
    def _():
        h2 = _rms(x1_ref[...], nw_ref[...], NORM_EPS)
        h2_ref[...] = h2
        qp = jnp.dot(h2.astype(bf16), wq_ref[...], preferred_element_type=f32).astype(bf16)
        for hh in range(PEER_HEADS):
            qp_ref[hh] = qp[:, hh * 2 * PEER_KEY_DIM:(hh + 1) * 2 * PEER_KEY_DIM]

    nt = (((1,), (1,)), ((), ()))
    tm = qp_ref.shape[1]
    for c0 in range(0, tm, LANES):
        cols = slice(c0, c0 + LANES)
        scores = []
        for g in range(ROUTE_HEADS):
            q = qp_ref[h * ROUTE_HEADS + g, cols, :]
            for i in range(2):
                scores.append(lax.dot_general(
                    keys_ref[2 * (h * ROUTE_HEADS + g) + i], q[:, i * PEER_KEY_DIM:(i + 1) * PEER_KEY_DIM], nt,
                    preferred_element_type=f32))
        halves = _topk_rows_lockstep(scores, PEER_TOPK)
        grids = [_candidate_grid(halves[2 * g][0], halves[2 * g + 1][0]) for g in range(ROUTE_HEADS)]
        flat_ids = grids[0][1]
        tops = _topk_rows_lockstep([c for c, _ in grids], PEER_TOPK, flat_ids, PEER_TOPK * PEER_TOPK)
        for g, (best, flat) in enumerate(tops):
            i1, i2 = halves[2 * g][1], halves[2 * g + 1][1]
            e1 = _select_rows(i1, lax.shift_right_logical(flat, int(math.log2(PEER_TOPK))))
            e2 = _select_rows(i2, lax.bitwise_and(flat, PEER_TOPK - 1))
            rows = slice(g * PEER_TOPK, (g + 1) * PEER_TOPK)
            e_ref[rows, cols] = e1 * N_KEYS + e2
            ex = jnp.exp(best - jnp.max(best, axis=0, keepdims=True))
            g_ref[rows, cols] = ex / jnp.sum(ex, axis=0, keepdims=True)


def _peer_route(x1, norm_w, wq_bf, keys_bf):
    T = x1.shape[0]
    tm = min(ROUTE_TM, T)
    return pl.pallas_call(
        _route_body,
        out_shape=(jax.ShapeDtypeStruct((T, D_MODEL), f32),
                   jax.ShapeDtypeStruct((PEER_SLOTS, T), jnp.int32),
                   jax.ShapeDtypeStruct((PEER_SLOTS, T), f32)),
        grid=(T // tm, PEER_HEADS // ROUTE_HEADS),
        in_specs=[pl.BlockSpec((tm, D_MODEL), lambda i, h: (i, 0)),
                  pl.BlockSpec((1, D_MODEL), lambda i, h: (0, 0)),
                  pl.BlockSpec((D_MODEL, 2 * PEER_KEY_DIM * PEER_HEADS), lambda i, h: (0, 0)),
                  pl.BlockSpec((2 * PEER_HEADS, N_KEYS, PEER_KEY_DIM), lambda i, h: (0, 0, 0))],
        out_specs=(pl.BlockSpec((tm, D_MODEL), lambda i, h: (i, 0)),
                   pl.BlockSpec((ROUTE_HEADS * PEER_TOPK, tm), lambda i, h: (h, i)),
                   pl.BlockSpec((ROUTE_HEADS * PEER_TOPK, tm), lambda i, h: (h, i))),
        scratch_shapes=[pltpu.VMEM((PEER_HEADS, tm, 2 * PEER_KEY_DIM), bf16)],
        compiler_params=_params("parallel", "arbitrary"),
        name="peer_route",
    )(x1, norm_w, wq_bf, keys_bf)


def _pack_uv(u, v):
    ub = lax.bitcast_convert_type(u.astype(bf16), jnp.uint16).astype(u32)
    vb = lax.bitcast_convert_type(v.astype(bf16), jnp.uint16).astype(u32)
    return ((ub << 16) | vb).reshape(u.shape[0], D_CHUNKS, LANES)


def _peer_body(idx_ref, idx_next_ref, gates_ref, h2_ref, x1_ref, fw_ref, uv_hbm, o_ref, *scratch, tt):
    bufs, (sem, w_ref, row_ref) = scratch[:PEER_NBUF], scratch[PEER_NBUF:]
    i = pl.program_id(0)
    last_step = pl.num_programs(0) - 1

    def issue(iref, tn, b, lo=0, hi=PEER_SLOTS):
        for j in range(lo, hi):
            pltpu.async_copy(uv_hbm.at[iref[tn, j]], bufs[b].at[:, j, :], sem.at[b], priority=j % 2)

    def wait_buf(b):
        pltpu.make_async_copy(uv_hbm.at[pl.ds(0, PEER_SLOTS)], bufs[b], sem.at[b]).wait()

    lane_t = lax.broadcasted_iota(jnp.int32, (PEER_SLOTS, tt), 1)
    hi_mask = jnp.asarray(0xFFFF0000, u32)

    per_chunk = PEER_SLOTS // D_CHUNKS
    chunk = lambda c: slice(c * LANES, (c + 1) * LANES)

    def act_chunk(b, hrow, c, acc):
        u = lax.bitcast_convert_type(bufs[b][c] & hi_mask, f32)
        return acc + u * hrow[:, chunk(c)]

    def finish_weights(t, acc):
        a = jnp.sum(acc, axis=1, keepdims=True)
        gate = jnp.sum(jnp.where(lane_t == t, gates_ref[...], 0.0), axis=1, keepdims=True)
        w = gate * (0.5 * a * (1.0 + lax.erf(a * (2.0 ** -0.5))))
        w_ref[...] = jnp.broadcast_to(w, (PEER_SLOTS, LANES))

    def step(t, b, iref, tn, has_next):
        b1 = (b + 1) % PEER_NBUF
        nb = (b + PEER_AHEAD) % PEER_NBUF
        if has_next:
            wait_buf(b1)
            hrow1 = h2_ref[pl.ds(t + 1, 1), :]
        wb = w_ref[...]
        xrow = x1_ref[pl.ds(t, 1), :]
        acc = jnp.zeros((PEER_SLOTS, LANES), f32)
        for c in range(D_CHUNKS):
            issue(iref, tn, nb, c * per_chunk, (c + 1) * per_chunk)
            if has_next:
                acc = act_chunk(b1, hrow1, c, acc)
            v = lax.bitcast_convert_type(bufs[b][c] << 16, f32)
            row_ref[:, chunk(c)] = xrow[:, chunk(c)] + jnp.sum(v * wb, axis=0, keepdims=True)
        if has_next:
            finish_weights(t + 1, acc)
        o_ref[pl.ds(t, 1), :] = _rms(row_ref[...], fw_ref[...], NORM_EPS)

    @pl.when(i == 0)
    def _():
        for t in range(PEER_AHEAD):
            issue(idx_ref, t, t)

    wait_buf(0)
    hrow0 = h2_ref[pl.ds(0, 1), :]
    acc0 = jnp.zeros((PEER_SLOTS, LANES), f32)
    for c in range(D_CHUNKS):
        acc0 = act_chunk(0, hrow0, c, acc0)
    finish_weights(0, acc0)

    groups = tt // PEER_NBUF

    def group(g, carry):
        for b in range(PEER_NBUF):
            t = g * PEER_NBUF + b
            step(t, b, idx_ref, t + PEER_AHEAD, True)
        return carry

    lax.fori_loop(0, groups - 1, group, 0)
    for b in range(PEER_NBUF):
        t = (groups - 1) * PEER_NBUF + b
        tn = t + PEER_AHEAD
        if tn < tt:
            step(t, b, idx_ref, tn, True)
        else:
            step(t, b, idx_next_ref, tn - tt, t + 1 < tt)

    @pl.when(i == last_step)
    def _():
        for t in range(PEER_AHEAD):
            wait_buf(t)


def _peer_apply(idx, gates_t, h2, x1, final_w, uv):
    T = x1.shape[0]
    tt = PEER_TT
    assert T % tt == 0 and tt % PEER_NBUF == 0 and tt >= 2 * PEER_NBUF
    nsteps = T // tt
    return pl.pallas_call(
        functools.partial(_peer_body, tt=tt),
        out_shape=jax.ShapeDtypeStruct((T, D_MODEL), f32),
        grid=(nsteps,),
        in_specs=[pl.BlockSpec((tt, PEER_SLOTS), lambda i: (i, 0), memory_space=pltpu.SMEM),
                  pl.BlockSpec((tt, PEER_SLOTS), lambda i: (jnp.minimum(i + 1, nsteps - 1), 0),
                               memory_space=pltpu.SMEM),
                  pl.BlockSpec((PEER_SLOTS, tt), lambda i: (0, i)),
                  pl.BlockSpec((tt, D_MODEL), lambda i: (i, 0)),
                  pl.BlockSpec((tt, D_MODEL), lambda i: (i, 0)),
                  pl.BlockSpec((1, D_MODEL), lambda i: (0, 0)),
                  pl.BlockSpec(memory_space=pl.ANY)],
        out_specs=pl.BlockSpec((tt, D_MODEL), lambda i: (i, 0)),
        scratch_shapes=[pltpu.VMEM((D_CHUNKS, PEER_SLOTS, LANES), u32) for _ in range(PEER_NBUF)]
        + [pltpu.SemaphoreType.DMA((PEER_NBUF,)),
           pltpu.VMEM((PEER_SLOTS, LANES), f32),
           pltpu.VMEM((1, D_MODEL), f32)],
        compiler_params=_params("arbitrary"),
        name="peer_apply",
    )(idx, idx, gates_t, h2, x1, final_w, uv)


def _trunk(x, rel_bias, norm1_w, w_in, lambda_q1, lambda_k1, lambda_q2, lambda_k2, subln_w,
           pool_w, pool_scale, w_out, norm2_w, peer_wq, peer_keys, peer_u, peer_v, final_norm_w):
    B, S, _ = x.shape
    T = B * S
    depth = norm1_w.shape[0]
    assert depth == 1, "the final norm is fused into the (single) layer's PEER stage"
    l = 0
    x2d = x.reshape(T, D_MODEL)
    lam_init = _lambda_init(l)
    lam = (jnp.exp(jnp.sum(lambda_q1[l].astype(f32) * lambda_k1[l].astype(f32)))
           - jnp.exp(jnp.sum(lambda_q2[l].astype(f32) * lambda_k2[l].astype(f32)))
           + lam_init).reshape(1)
    proj = _inproj(x2d, norm1_w[l].reshape(1, D_MODEL), w_in[l].astype(bf16))
    att = _attention(proj.reshape(B, S, IN_PROJ_DIM), lam, rel_bias, subln_w[l].reshape(1, ATT_V_DIM),
                     1.0 - lam_init)
    x1 = _pool_outproj(x2d, att.reshape(T, ATT_WIDTH), proj, pool_w[l].astype(bf16),
                       pool_scale[l].reshape(1, POOL_WIDTH), w_out[l].astype(bf16), S)
    keys = peer_keys[l].reshape(2 * PEER_HEADS, N_KEYS, PEER_KEY_DIM).astype(bf16)
    h2, expert_t, gates_t = _peer_route(x1, norm2_w[l].reshape(1, D_MODEL), peer_wq[l].astype(bf16), keys)
    y = _peer_apply(jnp.transpose(expert_t), gates_t, h2, x1, final_norm_w.reshape(1, D_MODEL),
                    _pack_uv(peer_u[l], peer_v[l]))
    return y.reshape(B, S, D_MODEL)


def kernel(x_prompt, x_sample, rel_bias, norm1_w, w_in, lambda_q1, lambda_k1, lambda_q2, lambda_k2, subln_w,
           pool_w, pool_scale, w_out, norm2_w, peer_wq, peer_keys, peer_u, peer_v, final_norm_w):
    args = (rel_bias, norm1_w, w_in, lambda_q1, lambda_k1, lambda_q2, lambda_k2, subln_w,
            pool_w, pool_scale, w_out, norm2_w, peer_wq, peer_keys, peer_u, peer_v, final_norm_w)
    return (_trunk(x_prompt, *args), _trunk(x_sample, *args))
```

```python
import functools
import math

import numpy as np
import jax
import jax.numpy as jnp
from jax import lax
from jax.experimental import pallas as pl
from jax.experimental.pallas import tpu as pltpu

f32 = jnp.float32
bf16 = jnp.bfloat16
u32 = jnp.uint32

D_MODEL = 2048
ATT_WIDTH = 1024
N_ATT_HEADS = 8
ATT_HEAD_DIM = 64
ATT_V_DIM = 128
POOL_WIDTH = 1024
POOL_WINDOWS = (2, 4, 8, 16)
POOL_GROUP_DIM = 256
IN_PROJ_DIM = 3 * ATT_WIDTH + POOL_WIDTH
N_BUCKETS = 32
MAX_DISTANCE = 128
N_KEYS = 128
PEER_HEADS = 8
PEER_TOPK = 16
PEER_KEY_DIM = 128
PEER_SLOTS = PEER_HEADS * PEER_TOPK
NORM_EPS = 1e-6
SUBLN_EPS = 1e-5
LOG2E = math.log2(math.e)

LANES = 128
BF16_SUBLANES = 16
VMEM_LIMIT = 56 * 1024 * 1024
D_CHUNKS = D_MODEL // LANES

ATT_TILE = 512
ATT_UNROLL = 4
INPROJ_TM, INPROJ_TN = 1024, 1024
POOL_TM = 256
ROUTE_TM = 256
ROUTE_HEADS = 4
PEER_TT = 128
PEER_NBUF = 8
PEER_AHEAD = PEER_NBUF - 1
POOL_HALO = BF16_SUBLANES


def _lambda_init(layer_idx):
    return 0.8 - 0.6 * math.exp(-0.3 * layer_idx)


def _rms(x, w, eps):
    ms = jnp.mean(x * x, axis=-1, keepdims=True)
    return x * lax.rsqrt(ms + eps) * w


def _params(*sem):
    return pltpu.CompilerParams(dimension_semantics=sem, vmem_limit_bytes=VMEM_LIMIT)


def _inproj_body(x_ref, nw_ref, w_ref, o_ref, h_ref):
    @pl.when(pl.program_id(1) == 0)
    def _():
        h_ref[...] = _rms(x_ref[...], nw_ref[...], NORM_EPS).astype(bf16)

    o_ref[...] = jnp.dot(h_ref[...], w_ref[...], preferred_element_type=f32).astype(o_ref.dtype)


def _inproj(x2d, norm_w, w_bf):
    T, N = x2d.shape[0], w_bf.shape[1]
    tm, tn = min(INPROJ_TM, T), INPROJ_TN
    return pl.pallas_call(
        _inproj_body,
        out_shape=jax.ShapeDtypeStruct((T, N), bf16),
        grid=(T // tm, N // tn),
        in_specs=[pl.BlockSpec((tm, D_MODEL), lambda i, j: (i, 0)),
                  pl.BlockSpec((1, D_MODEL), lambda i, j: (0, 0)),
                  pl.BlockSpec((D_MODEL, tn), lambda i, j: (0, j))],
        out_specs=pl.BlockSpec((tm, tn), lambda i, j: (i, j)),
        scratch_shapes=[pltpu.VMEM((tm, D_MODEL), bf16)],
        compiler_params=_params("parallel", "arbitrary"),
        name="inproj",
    )(x2d, norm_w, w_bf)


def _bucket_of_rel(rel):
    nb = N_BUCKETS // 2
    ret = (rel > 0).astype(np.int64) * nb
    n = np.abs(rel)
    max_exact = nb // 2
    large = max_exact + (np.log(np.maximum(n, 1).astype(np.float64) / max_exact)
                         / math.log(MAX_DISTANCE / max_exact) * (nb - max_exact)).astype(np.int64)
    large = np.minimum(large, nb - 1)
    return (ret + np.where(n < max_exact, n, large)).astype(np.int32)


def _attn_body(lam_ref, cfar_ref, q_ref, k_ref, v_ref, bias_ref, sw_ref, o_ref,
               qm_ref, m_ref, l_ref, acc_ref, *, lam_scale, t, nk):
    h, qi = pl.program_id(1), pl.program_id(2)
    m_ref[...] = jnp.full(m_ref.shape, -jnp.inf, f32)
    l_ref[...] = jnp.zeros(l_ref.shape, f32)
    acc_ref[...] = jnp.zeros(acc_ref.shape, f32)

    q = (q_ref[...].astype(f32) * (LOG2E * ATT_HEAD_DIM ** -0.5)).astype(bf16)
    lane = lax.broadcasted_iota(jnp.int32, q.shape, 1)
    qm_ref[:t] = jnp.where(lane < ATT_HEAD_DIM, q, jnp.zeros_like(q))
    qm_ref[t:] = jnp.where(lane >= ATT_HEAD_DIM, q, jnp.zeros_like(q))
    nt = (((1,), (1,)), ((), ()))

    def chunk(j, tile_d, c):
        start = pl.multiple_of(j * t, t)
        k = k_ref[pl.ds(start, t), :]
        v = v_ref[pl.ds(start, t), :]
        s = lax.dot_general(qm_ref[...], k, nt, preferred_element_type=f32)
        m_prev = m_ref[...]
        if tile_d is None:
            m_new = jnp.maximum(m_prev, jnp.max(s, axis=1, keepdims=True) + c)
            shift = m_new - c
        else:
            b = bias_ref[tile_d + 1]
            s = jnp.concatenate([s[:t] + b, s[t:] + b], axis=0)
            m_new = jnp.maximum(m_prev, jnp.max(s, axis=1, keepdims=True))
            shift = m_new
        alpha = jnp.exp2(m_prev - m_new)
        p = jnp.exp2(s - jnp.concatenate([shift] * (t // LANES), axis=1))
        l_ref[...] = alpha * l_ref[...] + jnp.sum(p, axis=1, keepdims=True)
        acc_ref[...] = alpha * acc_ref[...] + jnp.dot(p.astype(bf16), v, preferred_element_type=f32)
        m_ref[...] = m_new

    def far_sweep(lo, hi, c):
        groups = (hi - lo) // ATT_UNROLL

        def body(j, carry):
            for n in range(ATT_UNROLL):
                chunk(lo + ATT_UNROLL * j + n, None, c)
            return carry
        lax.fori_loop(0, groups, body, 0)

        done = lo + ATT_UNROLL * groups
        left = hi - done
        width = ATT_UNROLL // 2
        while width >= 1:
            @pl.when((left & width) != 0)
            def _(done=done, width=width):
                for n in range(width):
                    chunk(done + n, None, c)
            done = done + (left & width)
            width //= 2

    far_sweep(0, jnp.maximum(qi - 1, 0), cfar_ref[h, 0])
    for d in (-1, 0, 1):
        j = qi + d

        @pl.when(jnp.logical_and(j >= 0, j < nk))
        def _():
            chunk(j, d, None)
    far_sweep(jnp.minimum(qi + 2, nk), nk, cfar_ref[h, 1])

    lam = lam_ref[0]
    out = acc_ref[:t] / l_ref[:t] - lam * (acc_ref[t:] / l_ref[t:])
    o_ref[...] = (_rms(out, sw_ref[...], SUBLN_EPS) * lam_scale).astype(o_ref.dtype)


def _bias_tiles(rel_bias, t):
    span = 2 * t - 1
    rel_vec = np.arange(-span, span + 1)
    bias_vec = jnp.transpose(rel_bias[jnp.asarray(_bucket_of_rel(rel_vec))].astype(f32))
    tiles = []
    for d in (-1, 0, 1):
        w = lax.slice_in_dim(bias_vec, t + d * t, t + d * t + span, axis=1)
        wp = jnp.pad(w, ((0, 0), (0, 1)))
        skew = jnp.tile(wp, (1, t))[:, :t * span].reshape(-1, t, span)
        tiles.append(skew[:, :, t - 1:t - 1 + t])
    return jnp.stack(tiles, axis=1)


def _attention(proj3, lam, rel_bias, subln_w, lam_scale):
    B, S, _ = proj3.shape
    t = min(ATT_TILE, S)
    assert S % t == 0 and t >= MAX_DISTANCE
    bias_tiles = _bias_tiles(rel_bias, t) * LOG2E
    far = _bucket_of_rel(np.array([-MAX_DISTANCE, MAX_DISTANCE]))
    cfar = jnp.transpose(rel_bias[jnp.asarray(far)].astype(f32)) * LOG2E
    n_heads = N_ATT_HEADS
    nk = S // t
    return pl.pallas_call(
        functools.partial(_attn_body, lam_scale=lam_scale, t=t, nk=nk),
        out_shape=jax.ShapeDtypeStruct((B, S, ATT_WIDTH), bf16),
        grid=(B, n_heads, nk),
        in_specs=[pl.BlockSpec(memory_space=pltpu.SMEM),
                  pl.BlockSpec(memory_space=pltpu.SMEM),
                  pl.BlockSpec((None, t, LANES), lambda b, h, qi: (b, qi, h)),
                  pl.BlockSpec((None, S, LANES), lambda b, h, qi: (b, 0, n_heads + h)),
                  pl.BlockSpec((None, S, LANES), lambda b, h, qi: (b, 0, 2 * n_heads + h)),
                  pl.BlockSpec((None, 3, t, t), lambda b, h, qi: (h, 0, 0, 0)),
                  pl.BlockSpec((1, ATT_V_DIM), lambda b, h, qi: (0, 0))],
        out_specs=pl.BlockSpec((None, t, LANES), lambda b, h, qi: (b, qi, h)),
        scratch_shapes=[pltpu.VMEM((2 * t, LANES), bf16),
                        pltpu.VMEM((2 * t, LANES), f32), pltpu.VMEM((2 * t, LANES), f32),
                        pltpu.VMEM((2 * t, ATT_V_DIM), f32)],
        compiler_params=_params("parallel", "parallel", "arbitrary"),
        name="diff_attention",
    )(lam, cfar, proj3, proj3, proj3, bias_tiles, subln_w)


def _pool_body(x_ref, att_ref, p_ref, prev_ref, next_ref, pw_ref, ps_ref, wo_ref, o_ref, *, seq, tm):
    i = pl.program_id(0)
    t0 = (i * tm) % seq
    prev_ok = (t0 > 0).astype(f32)
    next_ok = (t0 + tm < seq).astype(f32)
    pm = p_ref[...].astype(f32)
    ext = jnp.concatenate([prev_ref[...].astype(f32) * prev_ok, pm,
                           next_ref[...].astype(f32) * next_ok], axis=0)
    pos = t0 + lax.broadcasted_iota(jnp.int32, (tm, 1), 0)
    pooled = []
    for g, w in enumerate(POOL_WINDOWS):
        cs = slice(g * POOL_GROUP_DIM, (g + 1) * POOL_GROUP_DIM)
        arr = ext[:, cs]
        n = 1
        while n < w:
            rows = arr.shape[0]
            arr = arr[0:rows - n] + arr[n:rows]
            n *= 2
        half = w // 2
        wsum = arr[POOL_HALO - half:POOL_HALO - half + tm]
        count = (jnp.minimum(pos + half, seq) - jnp.maximum(pos - half, 0)).astype(f32)
        mixed = (wsum / count - pm[:, cs]).astype(bf16)
        pooled.append((jnp.dot(mixed, pw_ref[g], preferred_element_type=f32) * ps_ref[:, cs]).astype(bf16))
    cat = jnp.concatenate([att_ref[...]] + pooled, axis=1)
    o_ref[...] = x_ref[...] + jnp.dot(cat, wo_ref[...], preferred_element_type=f32)


def _pool_outproj(x2d, att2d, proj, pool_w_bf, pool_scale, w_out_bf, seq):
    T = x2d.shape[0]
    tm = min(POOL_TM, seq)
    assert seq % tm == 0 and tm % POOL_HALO == 0
    hb = tm // POOL_HALO
    last = T // POOL_HALO - 1
    pcol = 3 * ATT_WIDTH // POOL_WIDTH
    return pl.pallas_call(
        functools.partial(_pool_body, seq=seq, tm=tm),
        out_shape=jax.ShapeDtypeStruct((T, D_MODEL), f32),
        grid=(T // tm,),
        in_specs=[pl.BlockSpec((tm, D_MODEL), lambda i: (i, 0)),
                  pl.BlockSpec((tm, ATT_WIDTH), lambda i: (i, 0)),
                  pl.BlockSpec((tm, POOL_WIDTH), lambda i: (i, pcol)),
                  pl.BlockSpec((POOL_HALO, POOL_WIDTH), lambda i: (jnp.maximum(i * hb - 1, 0), pcol)),
                  pl.BlockSpec((POOL_HALO, POOL_WIDTH), lambda i: (jnp.minimum((i + 1) * hb, last), pcol)),
                  pl.BlockSpec((len(POOL_WINDOWS), POOL_GROUP_DIM, POOL_GROUP_DIM), lambda i: (0, 0, 0)),
                  pl.BlockSpec((1, POOL_WIDTH), lambda i: (0, 0)),
                  pl.BlockSpec((D_MODEL, D_MODEL), lambda i: (0, 0))],
        out_specs=pl.BlockSpec((tm, D_MODEL), lambda i: (i, 0)),
        compiler_params=_params("parallel"),
        name="pool_outproj",
    )(x2d, att2d, proj, proj, proj, pool_w_bf, pool_scale, w_out_bf)


def _topk_rows(s, k, ids=None, id_bound=None):
    return _topk_rows_lockstep([s], k, ids, id_bound)[0]


def _topk_rows_lockstep(arrays, k, ids=None, id_bound=None):
    if ids is None:
        ids = lax.broadcasted_iota(jnp.int32, arrays[0].shape, 0)
        id_bound = arrays[0].shape[0]
    arrays = list(arrays)
    vals = [[] for _ in arrays]
    picked = [[] for _ in arrays]
    for _ in range(k):
        for n, s in enumerate(arrays):
            m = jnp.max(s, axis=0, keepdims=True)
            pick = jnp.minimum(jnp.min(jnp.where(s == m, ids, id_bound), axis=0, keepdims=True), id_bound - 1)
            vals[n].append(m)
            picked[n].append(pick)
            arrays[n] = jnp.where(ids == pick, -jnp.inf, s)
    return [(jnp.concatenate(v, axis=0), jnp.concatenate(p, axis=0)) for v, p in zip(vals, picked)]


def _candidate_grid(s1, s2):
    k, hk = PEER_TOPK, PEER_TOPK // 2
    m = s1.shape[1]
    iota_k = lax.broadcasted_iota(jnp.int32, (k, m), 0)
    iota_h = lax.broadcasted_iota(jnp.int32, (hk, m), 0)
    vals = [s1[0:1, :] + s2]
    ids = [iota_k]
    for a in range(1, hk):
        vals.append(s1[a:a + 1, :] + s2[0:hk, :])
        ids.append(iota_h + a * k)
    vals.append(s1[hk:k, :] + s2[0:1, :])
    ids.append((iota_h + hk) * k)
    return jnp.concatenate(vals, axis=0), jnp.concatenate(ids, axis=0)


def _select_rows(table, sel):
    out = jnp.zeros(sel.shape, table.dtype)
    for a in range(table.shape[0]):
        out = jnp.where(sel == a, table[a:a + 1, :], out)
    return out


def _route_body(x1_ref, nw_ref, wq_ref, keys_ref, h2_ref, e_ref, g_ref, qp_ref):
    h = pl.program_id(1)

    @pl.when(h == 0)
    def _():
        h2 = _rms(x1_ref[...], nw_ref[...], NORM_EPS)
        h2_ref[...] = h2
        qp = jnp.dot(h2.astype(bf16), wq_ref[...], preferred_element_type=f32).astype(bf16)
        for hh in range(PEER_HEADS):
            qp_ref[hh] = qp[:, hh * 2 * PEER_KEY_DIM:(hh + 1) * 2 * PEER_KEY_DIM]

    nt = (((1,), (1,)), ((), ()))
    tm = qp_ref.shape[1]
    for c0 in range(0, tm, LANES):
        cols = slice(c0, c0 + LANES)
        scores = []
        for g in range(ROUTE_HEADS):
            q = qp_ref[h * ROUTE_HEADS + g, cols, :]
            for i in range(2):
                scores.append(lax.dot_general(
                    keys_ref[2 * (h * ROUTE_HEADS + g) + i], q[:, i * PEER_KEY_DIM:(i + 1) * PEER_KEY_DIM], nt,
                    preferred_element_type=f32))
        halves = _topk_rows_lockstep(scores, PEER_TOPK)
        grids = [_candidate_grid(halves[2 * g][0], halves[2 * g + 1][0]) for g in range(ROUTE_HEADS)]
        flat_ids = grids[0][1]
        tops = _topk_rows_lockstep([c for c, _ in grids], PEER_TOPK, flat_ids, PEER_TOPK * PEER_TOPK)
        for g, (best, flat) in enumerate(tops):
            i1, i2 = halves[2 * g][1], halves[2 * g + 1][1]
            e1 = _select_rows(i1, lax.shift_right_logical(flat, int(math.log2(PEER_TOPK))))
            e2 = _select_rows(i2, lax.bitwise_and(flat, PEER_TOPK - 1))
            rows = slice(g * PEER_TOPK, (g + 1) * PEER_TOPK)
            e_ref[rows, cols] = e1 * N_KEYS + e2
            ex = jnp.exp(best - jnp.max(best, axis=0, keepdims=True))
            g_ref[rows, cols] = ex / jnp.sum(ex, axis=0, keepdims=True)


def _peer_route(x1, norm_w, wq_bf, keys_bf):
    T = x1.shape[0]
    tm = min(ROUTE_TM, T)
    return pl.pallas_call(
        _route_body,
        out_shape=(jax.ShapeDtypeStruct((T, D_MODEL), f32),
                   jax.ShapeDtypeStruct((PEER_SLOTS, T), jnp.int32),
                   jax.ShapeDtypeStruct((PEER_SLOTS, T), f32)),
        grid=(T // tm, PEER_HEADS // ROUTE_HEADS),
        in_specs=[pl.BlockSpec((tm, D_MODEL), lambda i, h: (i, 0)),
                  pl.BlockSpec((1, D_MODEL), lambda i, h: (0, 0)),
                  pl.BlockSpec((D_MODEL, 2 * PEER_KEY_DIM * PEER_HEADS), lambda i, h: (0, 0)),
                  pl.BlockSpec((2 * PEER_HEADS, N_KEYS, PEER_KEY_DIM), lambda i, h: (0, 0, 0))],
        out_specs=(pl.BlockSpec((tm, D_MODEL), lambda i, h: (i, 0)),
                   pl.BlockSpec((ROUTE_HEADS * PEER_TOPK, tm), lambda i, h: (h, i)),
                   pl.BlockSpec((ROUTE_HEADS * PEER_TOPK, tm), lambda i, h: (h, i))),
        scratch_shapes=[pltpu.VMEM((PEER_HEADS, tm, 2 * PEER_KEY_DIM), bf16)],
        compiler_params=_params("parallel", "arbitrary"),
        name="peer_route",
    )(x1, norm_w, wq_bf, keys_bf)


def _pack_uv(u, v):
    ub = lax.bitcast_convert_type(u.astype(bf16), jnp.uint16).astype(u32)
    vb = lax.bitcast_convert_type(v.astype(bf16), jnp.uint16).astype(u32)
    return ((ub << 16) | vb).reshape(u.shape[0], D_CHUNKS, LANES)


def _peer_body(idx_ref, idx_next_ref, gates_ref, h2_ref, x1_ref, fw_ref, uv_hbm, o_ref, *scratch, tt):
    bufs, (sem, w_ref, row_ref) = scratch[:PEER_NBUF], scratch[PEER_NBUF:]
    i = pl.program_id(0)
    last_step = pl.num_programs(0) - 1

    def issue(iref, tn, b, lo=0, hi=PEER_SLOTS):
        for j in range(lo, hi):
            pltpu.async_copy(uv_hbm.at[iref[tn, j]], bufs[b].at[:, j, :], sem.at[b], priority=j % 2)

    def wait_buf(b):
        pltpu.make_async_copy(uv_hbm.at[pl.ds(0, PEER_SLOTS)], bufs[b], sem.at[b]).wait()

    lane_t = lax.broadcasted_iota(jnp.int32, (PEER_SLOTS, tt), 1)
    hi_mask = jnp.asarray(0xFFFF0000, u32)

    per_chunk = PEER_SLOTS // D_CHUNKS
    chunk = lambda c: slice(c * LANES, (c + 1) * LANES)

    def act_chunk(b, hrow, c, acc):
        u = lax.bitcast_convert_type(bufs[b][c] & hi_mask, f32)
        return acc + u * hrow[:, chunk(c)]

    def finish_weights(t, acc):
        a = jnp.sum(acc, axis=1, keepdims=True)
        gate = jnp.sum(jnp.where(lane_t == t, gates_ref[...], 0.0), axis=1, keepdims=True)
        w = gate * (0.5 * a * (1.0 + lax.erf(a * (2.0 ** -0.5))))
        w_ref[...] = jnp.broadcast_to(w, (PEER_SLOTS, LANES))

    def step(t, b, iref, tn, has_next):
        b1 = (b + 1) % PEER_NBUF
        nb = (b + PEER_AHEAD) % PEER_NBUF
        if has_next:
            wait_buf(b1)
            hrow1 = h2_ref[pl.ds(t + 1, 1), :]
        wb = w_ref[...]
        xrow = x1_ref[pl.ds(t, 1), :]
        acc = jnp.zeros((PEER_SLOTS, LANES), f32)
        for c in range(D_CHUNKS):
            issue(iref, tn, nb, c * per_chunk, (c + 1) * per_chunk)
            if has_next:
                acc = act_chunk(b1, hrow1, c, acc)
            v = lax.bitcast_convert_type(bufs[b][c] << 16, f32)
            row_ref[:, chunk(c)] = xrow[:, chunk(c)] + jnp.sum(v * wb, axis=0, keepdims=True)
        if has_next:
            finish_weights(t + 1, acc)
        o_ref[pl.ds(t, 1), :] = _rms(row_ref[...], fw_ref[...], NORM_EPS)

    @pl.when(i == 0)
    def _():
        for t in range(PEER_AHEAD):
            issue(idx_ref, t, t)

    wait_buf(0)
    hrow0 = h2_ref[pl.ds(0, 1), :]
    acc0 = jnp.zeros((PEER_SLOTS, LANES), f32)
    for c in range(D_CHUNKS):
        acc0 = act_chunk(0, hrow0, c, acc0)
    finish_weights(0, acc0)

    groups = tt // PEER_NBUF

    def group(g, carry):
        for b in range(PEER_NBUF):
            t = g * PEER_NBUF + b
            step(t, b, idx_ref, t + PEER_AHEAD, True)
        return carry

    lax.fori_loop(0, groups - 1, group, 0)
    for b in range(PEER_NBUF):
        t = (groups - 1) * PEER_NBUF + b
        tn = t + PEER_AHEAD
        if tn < tt:
            step(t, b, idx_ref, tn, True)
        else:
            step(t, b, idx_next_ref, tn - tt, t + 1 < tt)

    @pl.when(i == last_step)
    def _():
        for t in range(PEER_AHEAD):
            wait_buf(t)


def _peer_apply(idx, gates_t, h2, x1, final_w, uv):
    T = x1.shape[0]
    tt = PEER_TT
    assert T % tt == 0 and tt % PEER_NBUF == 0 and tt >= 2 * PEER_NBUF
    nsteps = T // tt
    ahead_rows = 8
    assert PEER_AHEAD <= ahead_rows and tt % ahead_rows == 0
    return pl.pallas_call(
        functools.partial(_peer_body, tt=tt),
        out_shape=jax.ShapeDtypeStruct((T, D_MODEL), f32),
        grid=(nsteps,),
        in_specs=[pl.BlockSpec((tt, PEER_SLOTS), lambda i: (i, 0), memory_space=pltpu.SMEM),
                  pl.BlockSpec((ahead_rows, PEER_SLOTS),
                               lambda i: (jnp.minimum(i + 1, nsteps - 1) * (tt // ahead_rows), 0),
                               memory_space=pltpu.SMEM),
                  pl.BlockSpec((PEER_SLOTS, tt), lambda i: (0, i)),
                  pl.BlockSpec((tt, D_MODEL), lambda i: (i, 0)),
                  pl.BlockSpec((tt, D_MODEL), lambda i: (i, 0)),
                  pl.BlockSpec((1, D_MODEL), lambda i: (0, 0)),
                  pl.BlockSpec(memory_space=pl.ANY)],
        out_specs=pl.BlockSpec((tt, D_MODEL), lambda i: (i, 0)),
        scratch_shapes=[pltpu.VMEM((D_CHUNKS, PEER_SLOTS, LANES), u32) for _ in range(PEER_NBUF)]
        + [pltpu.SemaphoreType.DMA((PEER_NBUF,)),
           pltpu.VMEM((PEER_SLOTS, LANES), f32),
           pltpu.VMEM((1, D_MODEL), f32)],
        compiler_params=_params("arbitrary"),
        name="peer_apply",
    )(idx, idx, gates_t, h2, x1, final_w, uv)


def _trunk(x, rel_bias, norm1_w, w_in, lambda_q1, lambda_k1, lambda_q2, lambda_k2, subln_w,
           pool_w, pool_scale, w_out, norm2_w, peer_wq, peer_keys, peer_u, peer_v, final_norm_w):
    B, S, _ = x.shape
    T = B * S
    depth = norm1_w.shape[0]
    assert depth == 1, "the final norm is fused into the (single) layer's PEER stage"
    l = 0
    x2d = x.reshape(T, D_MODEL)
    lam_init = _lambda_init(l)
    lam = (jnp.exp(jnp.sum(lambda_q1[l].astype(f32) * lambda_k1[l].astype(f32)))
           - jnp.exp(jnp.sum(lambda_q2[l].astype(f32) * lambda_k2[l].astype(f32)))
           + lam_init).reshape(1)
    proj = _inproj(x2d, norm1_w[l].reshape(1, D_MODEL), w_in[l].astype(bf16))
    att = _attention(proj.reshape(B, S, IN_PROJ_DIM), lam, rel_bias, subln_w[l].reshape(1, ATT_V_DIM),
                     1.0 - lam_init)
    x1 = _pool_outproj(x2d, att.reshape(T, ATT_WIDTH), proj, pool_w[l].astype(bf16),
                       pool_scale[l].reshape(1, POOL_WIDTH), w_out[l].astype(bf16), S)
    keys = peer_keys[l].reshape(2 * PEER_HEADS, N_KEYS, PEER_KEY_DIM).astype(bf16)
    h2, expert_t, gates_t = _peer_route(x1, norm2_w[l].reshape(1, D_MODEL), peer_wq[l].astype(bf16), keys)
    y = _peer_apply(jnp.transpose(expert_t), gates_t, h2, x1, final_norm_w.reshape(1, D_MODEL),
                    _pack_uv(peer_u[l], peer_v[l]))
    return y.reshape(B, S, D_MODEL)


def kernel(x_prompt, x_sample, rel_bias, norm1_w, w_in, lambda_q1, lambda_k1, lambda_q2, lambda_k2, subln_w,
           pool_w, pool_scale, w_out, norm2_w, peer_wq, peer_keys, peer_u, peer_v, final_norm_w):
    args = (rel_bias, norm1_w, w_in, lambda_q1, lambda_k1, lambda_q2, lambda_k2, subln_w,
            pool_w, pool_scale, w_out, norm2_w, peer_wq, peer_keys, peer_u, peer_v, final_norm_w)
    return (_trunk(x_prompt, *args), _trunk(x_sample, *args))
```

```python
import functools
import math

import numpy as np
import jax
import jax.numpy as jnp
from jax import lax
from jax.experimental import pallas as pl
from jax.experimental.pallas import tpu as pltpu

f32 = jnp.float32
bf16 = jnp.bfloat16
u32 = jnp.uint32

D_MODEL = 2048
ATT_WIDTH = 1024
N_ATT_HEADS = 8
ATT_HEAD_DIM = 64
ATT_V_DIM = 128
POOL_WIDTH = 1024
POOL_WINDOWS = (2, 4, 8, 16)
POOL_GROUP_DIM = 256
IN_PROJ_DIM = 3 * ATT_WIDTH + POOL_WIDTH
N_BUCKETS = 32
MAX_DISTANCE = 128
N_KEYS = 128
PEER_HEADS = 8
PEER_TOPK = 16
PEER_KEY_DIM = 128
PEER_SLOTS = PEER_HEADS * PEER_TOPK
NORM_EPS = 1e-6
SUBLN_EPS = 1e-5
LOG2E = math.log2(math.e)

LANES = 128
BF16_SUBLANES = 16
VMEM_LIMIT = 56 * 1024 * 1024
D_CHUNKS = D_MODEL // LANES

ATT_TILE = 512
ATT_UNROLL = 4
INPROJ_TM, INPROJ_TN = 1024, 2048
POOL_TM = 512
ROUTE_TM = 256
ROUTE_HEADS = 4
PEER_TT = 128
PEER_NBUF = 8
PEER_AHEAD = PEER_NBUF - 1
POOL_HALO = BF16_SUBLANES


def _lambda_init(layer_idx):
    return 0.8 - 0.6 * math.exp(-0.3 * layer_idx)


def _rms(x, w, eps):
    ms = jnp.mean(x * x, axis=-1, keepdims=True)
    return x * lax.rsqrt(ms + eps) * w


def _params(*sem):
    return pltpu.CompilerParams(dimension_semantics=sem, vmem_limit_bytes=VMEM_LIMIT)


def _inproj_body(x_ref, nw_ref, w_ref, o_ref, h_ref):
    @pl.when(pl.program_id(1) == 0)
    def _():
        h_ref[...] = _rms(x_ref[...], nw_ref[...], NORM_EPS).astype(bf16)

    o_ref[...] = jnp.dot(h_ref[...], w_ref[...], preferred_element_type=f32).astype(o_ref.dtype)


def _inproj(x2d, norm_w, w_bf):
    T, N = x2d.shape[0], w_bf.shape[1]
    tm, tn = min(INPROJ_TM, T), INPROJ_TN
    return pl.pallas_call(
        _inproj_body,
        out_shape=jax.ShapeDtypeStruct((T, N), bf16),
        grid=(T // tm, N // tn),
        in_specs=[pl.BlockSpec((tm, D_MODEL), lambda i, j: (i, 0)),
                  pl.BlockSpec((1, D_MODEL), lambda i, j: (0, 0)),
                  pl.BlockSpec((D_MODEL, tn), lambda i, j: (0, j))],
        out_specs=pl.BlockSpec((tm, tn), lambda i, j: (i, j)),
        scratch_shapes=[pltpu.VMEM((tm, D_MODEL), bf16)],
        compiler_params=_params("parallel", "arbitrary"),
        name="inproj",
    )(x2d, norm_w, w_bf)


def _bucket_of_rel(rel):
    nb = N_BUCKETS // 2
    ret = (rel > 0).astype(np.int64) * nb
    n = np.abs(rel)
    max_exact = nb // 2
    large = max_exact + (np.log(np.maximum(n, 1).astype(np.float64) / max_exact)
                         / math.log(MAX_DISTANCE / max_exact) * (nb - max_exact)).astype(np.int64)
    large = np.minimum(large, nb - 1)
    return (ret + np.where(n < max_exact, n, large)).astype(np.int32)


def _attn_body(lam_ref, cfar_ref, q_ref, k_ref, v_ref, bias_ref, sw_ref, o_ref,
               qm_ref, m_ref, l_ref, acc_ref, *, lam_scale, t, nk):
    h, qi = pl.program_id(1), pl.program_id(2)
    m_ref[...] = jnp.full(m_ref.shape, -jnp.inf, f32)
    l_ref[...] = jnp.zeros(l_ref.shape, f32)
    acc_ref[...] = jnp.zeros(acc_ref.shape, f32)

    q = (q_ref[...].astype(f32) * (LOG2E * ATT_HEAD_DIM ** -0.5)).astype(bf16)
    lane = lax.broadcasted_iota(jnp.int32, q.shape, 1)
    qm_ref[:t] = jnp.where(lane < ATT_HEAD_DIM, q, jnp.zeros_like(q))
    qm_ref[t:] = jnp.where(lane >= ATT_HEAD_DIM, q, jnp.zeros_like(q))
    nt = (((1,), (1,)), ((), ()))

    def chunk(j, tile_d, c):
        start = pl.multiple_of(j * t, t)
        k = k_ref[pl.ds(start, t), :]
        v = v_ref[pl.ds(start, t), :]
        s = lax.dot_general(qm_ref[...], k, nt, preferred_element_type=f32)
        m_prev = m_ref[...]
        if tile_d is None:
            m_new = jnp.maximum(m_prev, jnp.max(s, axis=1, keepdims=True) + c)
            shift = m_new - c
        else:
            b = bias_ref[tile_d + 1]
            s = jnp.concatenate([s[:t] + b, s[t:] + b], axis=0)
            m_new = jnp.maximum(m_prev, jnp.max(s, axis=1, keepdims=True))
            shift = m_new
        alpha = jnp.exp2(m_prev - m_new)
        p = jnp.exp2(s - jnp.concatenate([shift] * (t // LANES), axis=1))
        l_ref[...] = alpha * l_ref[...] + jnp.sum(p, axis=1, keepdims=True)
        acc_ref[...] = alpha * acc_ref[...] + jnp.dot(p.astype(bf16), v, preferred_element_type=f32)
        m_ref[...] = m_new

    def far_sweep(lo, hi, c):
        groups = (hi - lo) // ATT_UNROLL

        def body(j, carry):
            for n in range(ATT_UNROLL):
                chunk(lo + ATT_UNROLL * j + n, None, c)
            return carry
        lax.fori_loop(0, groups, body, 0)

        done = lo + ATT_UNROLL * groups
        left = hi - done
        width = ATT_UNROLL // 2
        while width >= 1:
            @pl.when((left & width) != 0)
            def _(done=done, width=width):
                for n in range(width):
                    chunk(done + n, None, c)
            done = done + (left & width)
            width //= 2

    far_sweep(0, jnp.maximum(qi - 1, 0), cfar_ref[h, 0])
    for d in (-1, 0, 1):
        j = qi + d

        @pl.when(jnp.logical_and(j >= 0, j < nk))
        def _():
            chunk(j, d, None)
    far_sweep(jnp.minimum(qi + 2, nk), nk, cfar_ref[h, 1])

    lam = lam_ref[0]
    out = acc_ref[:t] / l_ref[:t] - lam * (acc_ref[t:] / l_ref[t:])
    o_ref[...] = (_rms(out, sw_ref[...], SUBLN_EPS) * lam_scale).astype(o_ref.dtype)


def _bias_tiles(rel_bias, t):
    span = 2 * t - 1
    rel_vec = np.arange(-span, span + 1)
    bias_vec = jnp.transpose(rel_bias[jnp.asarray(_bucket_of_rel(rel_vec))].astype(f32))
    tiles = []
    for d in (-1, 0, 1):
        w = lax.slice_in_dim(bias_vec, t + d * t, t + d * t + span, axis=1)
        wp = jnp.pad(w, ((0, 0), (0, 1)))
        skew = jnp.tile(wp, (1, t))[:, :t * span].reshape(-1, t, span)
        tiles.append(skew[:, :, t - 1:t - 1 + t])
    return jnp.stack(tiles, axis=1)


def _attention(proj3, lam, rel_bias, subln_w, lam_scale):
    B, S, _ = proj3.shape
    t = min(ATT_TILE, S)
    assert S % t == 0 and t >= MAX_DISTANCE
    bias_tiles = _bias_tiles(rel_bias, t) * LOG2E
    far = _bucket_of_rel(np.array([-MAX_DISTANCE, MAX_DISTANCE]))
    cfar = jnp.transpose(rel_bias[jnp.asarray(far)].astype(f32)) * LOG2E
    n_heads = N_ATT_HEADS
    nk = S // t
    return pl.pallas_call(
        functools.partial(_attn_body, lam_scale=lam_scale, t=t, nk=nk),
        out_shape=jax.ShapeDtypeStruct((B, S, ATT_WIDTH), bf16),
        grid=(B, n_heads, nk),
        in_specs=[pl.BlockSpec(memory_space=pltpu.SMEM),
                  pl.BlockSpec(memory_space=pltpu.SMEM),
                  pl.BlockSpec((None, t, LANES), lambda b, h, qi: (b, qi, h)),
                  pl.BlockSpec((None, S, LANES), lambda b, h, qi: (b, 0, n_heads + h)),
                  pl.BlockSpec((None, S, LANES), lambda b, h, qi: (b, 0, 2 * n_heads + h)),
                  pl.BlockSpec((None, 3, t, t), lambda b, h, qi: (h, 0, 0, 0)),
                  pl.BlockSpec((1, ATT_V_DIM), lambda b, h, qi: (0, 0))],
        out_specs=pl.BlockSpec((None, t, LANES), lambda b, h, qi: (b, qi, h)),
        scratch_shapes=[pltpu.VMEM((2 * t, LANES), bf16),
                        pltpu.VMEM((2 * t, LANES), f32), pltpu.VMEM((2 * t, LANES), f32),
                        pltpu.VMEM((2 * t, ATT_V_DIM), f32)],
        compiler_params=_params("parallel", "parallel", "arbitrary"),
        name="diff_attention",
    )(lam, cfar, proj3, proj3, proj3, bias_tiles, subln_w)


def _pool_body(x_ref, att_ref, p_ref, prev_ref, next_ref, pw_ref, ps_ref, wo_ref, o_ref, *, seq, tm):
    i = pl.program_id(0)
    t0 = (i * tm) % seq
    prev_ok = (t0 > 0).astype(f32)
    next_ok = (t0 + tm < seq).astype(f32)
    pm = p_ref[...].astype(f32)
    ext = jnp.concatenate([prev_ref[...].astype(f32) * prev_ok, pm,
                           next_ref[...].astype(f32) * next_ok], axis=0)
    pos = t0 + lax.broadcasted_iota(jnp.int32, (tm, 1), 0)
    pooled = []
    for g, w in enumerate(POOL_WINDOWS):
        cs = slice(g * POOL_GROUP_DIM, (g + 1) * POOL_GROUP_DIM)
        arr = ext[:, cs]
        n = 1
        while n < w:
            rows = arr.shape[0]
            arr = arr[0:rows - n] + arr[n:rows]
            n *= 2
        half = w // 2
        wsum = arr[POOL_HALO - half:POOL_HALO - half + tm]
        count = (jnp.minimum(pos + half, seq) - jnp.maximum(pos - half, 0)).astype(f32)
        mixed = (wsum / count - pm[:, cs]).astype(bf16)
        pooled.append((jnp.dot(mixed, pw_ref[g], preferred_element_type=f32) * ps_ref[:, cs]).astype(bf16))
    cat = jnp.concatenate([att_ref[...]] + pooled, axis=1)
    o_ref[...] = x_ref[...] + jnp.dot(cat, wo_ref[...], preferred_element_type=f32)


def _pool_outproj(x2d, att2d, proj, pool_w_bf, pool_scale, w_out_bf, seq):
    T = x2d.shape[0]
    tm = min(POOL_TM, seq)
    assert seq % tm == 0 and tm % POOL_HALO == 0
    hb = tm // POOL_HALO
    last = T // POOL_HALO - 1
    pcol = 3 * ATT_WIDTH // POOL_WIDTH
    return pl.pallas_call(
        functools.partial(_pool_body, seq=seq, tm=tm),
        out_shape=jax.ShapeDtypeStruct((T, D_MODEL), f32),
        grid=(T // tm,),
        in_specs=[pl.BlockSpec((tm, D_MODEL), lambda i: (i, 0)),
                  pl.BlockSpec((tm, ATT_WIDTH), lambda i: (i, 0)),
                  pl.BlockSpec((tm, POOL_WIDTH), lambda i: (i, pcol)),
                  pl.BlockSpec((POOL_HALO, POOL_WIDTH), lambda i: (jnp.maximum(i * hb - 1, 0), pcol)),
                  pl.BlockSpec((POOL_HALO, POOL_WIDTH), lambda i: (jnp.minimum((i + 1) * hb, last), pcol)),
                  pl.BlockSpec((len(POOL_WINDOWS), POOL_GROUP_DIM, POOL_GROUP_DIM), lambda i: (0, 0, 0)),
                  pl.BlockSpec((1, POOL_WIDTH), lambda i: (0, 0)),
                  pl.BlockSpec((D_MODEL, D_MODEL), lambda i: (0, 0))],
        out_specs=pl.BlockSpec((tm, D_MODEL), lambda i: (i, 0)),
        compiler_params=_params("parallel"),
        name="pool_outproj",
    )(x2d, att2d, proj, proj, proj, pool_w_bf, pool_scale, w_out_bf)


def _topk_rows(s, k, ids=None, id_bound=None):
    return _topk_rows_lockstep([s], k, ids, id_bound)[0]


def _topk_rows_lockstep(arrays, k, ids=None, id_bound=None):
    if ids is None:
        ids = lax.broadcasted_iota(jnp.int32, arrays[0].shape, 0)
        id_bound = arrays[0].shape[0]
    arrays = list(arrays)
    vals = [[] for _ in arrays]
    picked = [[] for _ in arrays]
    for _ in range(k):
        for n, s in enumerate(arrays):
            m = jnp.max(s, axis=0, keepdims=True)
            pick = jnp.minimum(jnp.min(jnp.where(s == m, ids, id_bound), axis=0, keepdims=True), id_bound - 1)
            vals[n].append(m)
            picked[n].append(pick)
            arrays[n] = jnp.where(ids == pick, -jnp.inf, s)
    return [(jnp.concatenate(v, axis=0), jnp.concatenate(p, axis=0)) for v, p in zip(vals, picked)]


def _candidate_grid(s1, s2):
    k, hk = PEER_TOPK, PEER_TOPK // 2
    m = s1.shape[1]
    iota_k = lax.broadcasted_iota(jnp.int32, (k, m), 0)
    iota_h = lax.broadcasted_iota(jnp.int32, (hk, m), 0)
    vals = [s1[0:1, :] + s2]
    ids = [iota_k]
    for a in range(1, hk):
        vals.append(s1[a:a + 1, :] + s2[0:hk, :])
        ids.append(iota_h + a * k)
    vals.append(s1[hk:k, :] + s2[0:1, :])
    ids.append((iota_h + hk) * k)
    return jnp.concatenate(vals, axis=0), jnp.concatenate(ids, axis=0)


def _select_rows(table, sel):
    out = jnp.zeros(sel.shape, table.dtype)
    for a in range(table.shape[0]):
        out = jnp.where(sel == a, table[a:a + 1, :], out)
    return out


def _route_body(x1_ref, nw_ref, wq_ref, keys_ref, h2_ref, e_ref, g_ref, qp_ref):
    h = pl.program_id(1)

    @pl.when(h == 0)
    def _():
        h2 = _rms(x1_ref[...], nw_ref[...], NORM_EPS)
        h2_ref[...] = h2
        qp = jnp.dot(h2.astype(bf16), wq_ref[...], preferred_element_type=f32).astype(bf16)
        for hh in range(PEER_HEADS):
            qp_ref[hh] = qp[:, hh * 2 * PEER_KEY_DIM:(hh + 1) * 2 * PEER_KEY_DIM]

    nt = (((1,), (1,)), ((), ()))
    tm = qp_ref.shape[1]
    for c0 in range(0, tm, LANES):
        cols = slice(c0, c0 + LANES)
        scores = []
        for g in range(ROUTE_HEADS):
            q = qp_ref[h * ROUTE_HEADS + g, cols, :]
            for i in range(2):
                scores.append(lax.dot_general(
                    keys_ref[2 * (h * ROUTE_HEADS + g) + i], q[:, i * PEER_KEY_DIM:(i + 1) * PEER_KEY_DIM], nt,
                    preferred_element_type=f32))
        halves = _topk_rows_lockstep(scores, PEER_TOPK)
        grids = [_candidate_grid(halves[2 * g][0], halves[2 * g + 1][0]) for g in range(ROUTE_HEADS)]
        flat_ids = grids[0][1]
        tops = _topk_rows_lockstep([c for c, _ in grids], PEER_TOPK, flat_ids, PEER_TOPK * PEER_TOPK)
        for g, (best, flat) in enumerate(tops):
            i1, i2 = halves[2 * g][1], halves[2 * g + 1][1]
            e1 = _select_rows(i1, lax.shift_right_logical(flat, int(math.log2(PEER_TOPK))))
            e2 = _select_rows(i2, lax.bitwise_and(flat, PEER_TOPK - 1))
            rows = slice(g * PEER_TOPK, (g + 1) * PEER_TOPK)
            e_ref[rows, cols] = e1 * N_KEYS + e2
            ex = jnp.exp(best - jnp.max(best, axis=0, keepdims=True))
            g_ref[rows, cols] = ex / jnp.sum(ex, axis=0, keepdims=True)


def _peer_route(x1, norm_w, wq_bf, keys_bf):
    T = x1.shape[0]
    tm = min(ROUTE_TM, T)
    return pl.pallas_call(
        _route_body,
        out_shape=(jax.ShapeDtypeStruct((T, D_MODEL), f32),
                   jax.ShapeDtypeStruct((PEER_SLOTS, T), jnp.int32),
                   jax.ShapeDtypeStruct((PEER_SLOTS, T), f32)),
        grid=(T // tm, PEER_HEADS // ROUTE_HEADS),
        in_specs=[pl.BlockSpec((tm, D_MODEL), lambda i, h: (i, 0)),
                  pl.BlockSpec((1, D_MODEL), lambda i, h: (0, 0)),
                  pl.BlockSpec((D_MODEL, 2 * PEER_KEY_DIM * PEER_HEADS), lambda i, h: (0, 0)),
                  pl.BlockSpec((2 * PEER_HEADS, N_KEYS, PEER_KEY_DIM), lambda i, h: (0, 0, 0))],
        out_specs=(pl.BlockSpec((tm, D_MODEL), lambda i, h: (i, 0)),
                   pl.BlockSpec((ROUTE_HEADS * PEER_TOPK, tm), lambda i, h: (h, i)),
                   pl.BlockSpec((ROUTE_HEADS * PEER_TOPK, tm), lambda i, h: (h, i))),
        scratch_shapes=[pltpu.VMEM((PEER_HEADS, tm, 2 * PEER_KEY_DIM), bf16)],
        compiler_params=_params("parallel", "arbitrary"),
        name="peer_route",
    )(x1, norm_w, wq_bf, keys_bf)


def _pack_uv(u, v):
    ub = lax.bitcast_convert_type(u.astype(bf16), jnp.uint16).astype(u32)
    vb = lax.bitcast_convert_type(v.astype(bf16), jnp.uint16).astype(u32)
    return ((ub << 16) | vb).reshape(u.shape[0], D_CHUNKS, LANES)


def _peer_body(idx_ref, idx_next_ref, gates_ref, h2_ref, x1_ref, fw_ref, uv_hbm, o_ref, *scratch, tt):
    bufs, (sem, w_ref, row_ref) = scratch[:PEER_NBUF], scratch[PEER_NBUF:]
    i = pl.program_id(0)
    last_step = pl.num_programs(0) - 1

    def issue(iref, tn, b, lo=0, hi=PEER_SLOTS):
        for j in range(lo, hi):
            pltpu.async_copy(uv_hbm.at[iref[tn, j]], bufs[b].at[:, j, :], sem.at[b], priority=j % 2)

    def wait_buf(b):
        pltpu.make_async_copy(uv_hbm.at[pl.ds(0, PEER_SLOTS)], bufs[b], sem.at[b]).wait()

    lane_t = lax.broadcasted_iota(jnp.int32, (PEER_SLOTS, tt), 1)
    hi_mask = jnp.asarray(0xFFFF0000, u32)

    per_chunk = PEER_SLOTS // D_CHUNKS
    chunk = lambda c: slice(c * LANES, (c + 1) * LANES)

    def act_chunk(b, hrow, c, acc):
        u = lax.bitcast_convert_type(bufs[b][c] & hi_mask, f32)
        return acc + u * hrow[:, chunk(c)]

    def finish_weights(t, acc):
        a = jnp.sum(acc, axis=1, keepdims=True)
        gate = jnp.sum(jnp.where(lane_t == t, gates_ref[...], 0.0), axis=1, keepdims=True)
        w = gate * (0.5 * a * (1.0 + lax.erf(a * (2.0 ** -0.5))))
        w_ref[...] = jnp.broadcast_to(w, (PEER_SLOTS, LANES))

    def step(t, b, iref, tn, has_next):
        b1 = (b + 1) % PEER_NBUF
        nb = (b + PEER_AHEAD) % PEER_NBUF
        if has_next:
            wait_buf(b1)
            hrow1 = h2_ref[pl.ds(t + 1, 1), :]
        wb = w_ref[...]
        xrow = x1_ref[pl.ds(t, 1), :]
        acc = jnp.zeros((PEER_SLOTS, LANES), f32)
        for c in range(D_CHUNKS):
            issue(iref, tn, nb, c * per_chunk, (c + 1) * per_chunk)
            if has_next:
                acc = act_chunk(b1, hrow1, c, acc)
            v = lax.bitcast_convert_type(bufs[b][c] << 16, f32)
            row_ref[:, chunk(c)] = xrow[:, chunk(c)] + jnp.sum(v * wb, axis=0, keepdims=True)
        if has_next:
            finish_weights(t + 1, acc)
        o_ref[pl.ds(t, 1), :] = _rms(row_ref[...], fw_ref[...], NORM_EPS)

    @pl.when(i == 0)
    def _():
        for t in range(PEER_AHEAD):
            issue(idx_ref, t, t)

    wait_buf(0)
    hrow0 = h2_ref[pl.ds(0, 1), :]
    acc0 = jnp.zeros((PEER_SLOTS, LANES), f32)
    for c in range(D_CHUNKS):
        acc0 = act_chunk(0, hrow0, c, acc0)
    finish_weights(0, acc0)

    groups = tt // PEER_NBUF

    def group(g, carry):
        for b in range(PEER_NBUF):
            t = g * PEER_NBUF + b
            step(t, b, idx_ref, t + PEER_AHEAD, True)
        return carry

    lax.fori_loop(0, groups - 1, group, 0)
    for b in range(PEER_NBUF):
        t = (groups - 1) * PEER_NBUF + b
        tn = t + PEER_AHEAD
        if tn < tt:
            step(t, b, idx_ref, tn, True)
        else:
            step(t, b, idx_next_ref, tn - tt, t + 1 < tt)

    @pl.when(i == last_step)
    def _():
        for t in range(PEER_AHEAD):
            wait_buf(t)


def _peer_apply(idx, gates_t, h2, x1, final_w, uv):
    T = x1.shape[0]
    tt = PEER_TT
    assert T % tt == 0 and tt % PEER_NBUF == 0 and tt >= 2 * PEER_NBUF
    nsteps = T // tt
    ahead_rows = 8
    assert PEER_AHEAD <= ahead_rows and tt % ahead_rows == 0
    return pl.pallas_call(
        functools.partial(_peer_body, tt=tt),
        out_shape=jax.ShapeDtypeStruct((T, D_MODEL), f32),
        grid=(nsteps,),
        in_specs=[pl.BlockSpec((tt, PEER_SLOTS), lambda i: (i, 0), memory_space=pltpu.SMEM),
                  pl.BlockSpec((ahead_rows, PEER_SLOTS),
                               lambda i: (jnp.minimum(i + 1, nsteps - 1) * (tt // ahead_rows), 0),
                               memory_space=pltpu.SMEM),
                  pl.BlockSpec((PEER_SLOTS, tt), lambda i: (0, i)),
                  pl.BlockSpec((tt, D_MODEL), lambda i: (i, 0)),
                  pl.BlockSpec((tt, D_MODEL), lambda i: (i, 0)),
                  pl.BlockSpec((1, D_MODEL), lambda i: (0, 0)),
                  pl.BlockSpec(memory_space=pl.ANY)],
        out_specs=pl.BlockSpec((tt, D_MODEL), lambda i: (i, 0)),
        scratch_shapes=[pltpu.VMEM((D_CHUNKS, PEER_SLOTS, LANES), u32) for _ in range(PEER_NBUF)]
        + [pltpu.SemaphoreType.DMA((PEER_NBUF,)),
           pltpu.VMEM((PEER_SLOTS, LANES), f32),
           pltpu.VMEM((1, D_MODEL), f32)],
        compiler_params=_params("arbitrary"),
        name="peer_apply",
    )(idx, idx, gates_t, h2, x1, final_w, uv)


def _trunk(x, rel_bias, norm1_w, w_in, lambda_q1, lambda_k1, lambda_q2, lambda_k2, subln_w,
           pool_w, pool_scale, w_out, norm2_w, peer_wq, peer_keys, peer_u, peer_v, final_norm_w):
    B, S, _ = x.shape
    T = B * S
    depth = norm1_w.shape[0]
    assert depth == 1, "the final norm is fused into the (single) layer's PEER stage"
    l = 0
    x2d = x.reshape(T, D_MODEL)
    lam_init = _lambda_init(l)
    lam = (jnp.exp(jnp.sum(lambda_q1[l].astype(f32) * lambda_k1[l].astype(f32)))
           - jnp.exp(jnp.sum(lambda_q2[l].astype(f32) * lambda_k2[l].astype(f32)))
           + lam_init).reshape(1)
    proj = _inproj(x2d, norm1_w[l].reshape(1, D_MODEL), w_in[l].astype(bf16))
    att = _attention(proj.reshape(B, S, IN_PROJ_DIM), lam, rel_bias, subln_w[l].reshape(1, ATT_V_DIM),
                     1.0 - lam_init)
    x1 = _pool_outproj(x2d, att.reshape(T, ATT_WIDTH), proj, pool_w[l].astype(bf16),
                       pool_scale[l].reshape(1, POOL_WIDTH), w_out[l].astype(bf16), S)
    keys = peer_keys[l].reshape(2 * PEER_HEADS, N_KEYS, PEER_KEY_DIM).astype(bf16)
    h2, expert_t, gates_t = _peer_route(x1, norm2_w[l].reshape(1, D_MODEL), peer_wq[l].astype(bf16), keys)
    y = _peer_apply(jnp.transpose(expert_t), gates_t, h2, x1, final_norm_w.reshape(1, D_MODEL),
                    _pack_uv(peer_u[l], peer_v[l]))
    return y.reshape(B, S, D_MODEL)


def kernel(x_prompt, x_sample, rel_bias, norm1_w, w_in, lambda_q1, lambda_k1, lambda_q2, lambda_k2, subln_w,
           pool_w, pool_scale, w_out, norm2_w, peer_wq, peer_keys, peer_u, peer_v, final_norm_w):
    args = (rel_bias, norm1_w, w_in, lambda_q1, lambda_k1, lambda_q2, lambda_k2, subln_w,
            pool_w, pool_scale, w_out, norm2_w, peer_wq, peer_keys, peer_u, peer_v, final_norm_w)
    return (_trunk(x_prompt, *args), _trunk(x_sample, *args))
```
